```python
import jax, jax.numpy as jnp
from jax import lax
import numpy as np

D_MODEL = 2048
BATCH = 1
SEQ = 16384
DEPTH = 4

MLSTM_HEADS = 4
MLSTM_DK = 128
MLSTM_DV = 256
CONV_WIDTH = 4
GLA_HEADS = 4
GLA_DK = 128
GLA_DV = 256
GLA_GATE_RANK = 16
GLA_GATE_TAU = 16.0
CHUNK = 64
ATTN_HEADS = 16
ATTN_KV_HEADS = 4
ATTN_DH = 128
ROPE_DIM = ATTN_DH // 4
ROPE_THETA = 500000.0
IDX_HEADS = 16
IDX_DH = 64
IDX_ROPE_DIM = IDX_DH // 4
INDEX_TOPK = 256
Q_BLOCK = 128
N_EXPERTS = 16
N_GROUPS = 4
EXPERTS_PER_GROUP = N_EXPERTS // N_GROUPS
GROUP_SCORE_TOPK = 2
TOP_K_EXPERTS = 2
EXPERT_FF = 512
DEEPNORM_ALPHA = (2 * DEPTH) ** 0.25
DEEPNORM_BETA = (8 * DEPTH) ** -0.25
LN_EPS = 1e-5
HEAD_NORM_EPS = 1e-6
MAX_POS_OFFSET = 4096

N_EVEN = (DEPTH + 1) // 2
N_ODD = DEPTH // 2
EVEN_SPLITS = (MLSTM_HEADS * MLSTM_DK, MLSTM_HEADS * MLSTM_DK, MLSTM_HEADS * MLSTM_DV, MLSTM_HEADS, MLSTM_HEADS, MLSTM_HEADS * MLSTM_DV, GLA_HEADS * GLA_DK, GLA_HEADS * GLA_DK, GLA_HEADS * GLA_DV, GLA_GATE_RANK, GLA_HEADS * GLA_DV)
EVEN_IN = sum(EVEN_SPLITS)
MIX_WIDTH = MLSTM_HEADS * MLSTM_DV + GLA_HEADS * GLA_DV
ODD_SPLITS = (ATTN_HEADS * ATTN_DH, ATTN_KV_HEADS * ATTN_DH, ATTN_KV_HEADS * ATTN_DH, IDX_HEADS * IDX_DH, IDX_DH, IDX_HEADS)
ODD_IN = sum(ODD_SPLITS)
ATTN_WIDTH = ATTN_HEADS * ATTN_DH

kernel_name = 'hybrid_mlstm_gla_dsa_moe_deepnorm'


def split_cols(y, sizes):
    cuts = [int(c) for c in np.cumsum(sizes)[:-1]]
    return jnp.split(y, cuts, axis=-1)


def layer_norm(x, gain, bias):
    xf = x.astype(jnp.float32)
    mu = jnp.mean(xf, axis=-1, keepdims=True)
    var = jnp.mean(jnp.square(xf - mu), axis=-1, keepdims=True)
    y = (xf - mu) * lax.rsqrt(var + LN_EPS) * gain.astype(jnp.float32) + bias.astype(jnp.float32)
    return y.astype(x.dtype)


def head_rms_norm(h, gain):
    hn = h * lax.rsqrt(jnp.mean(jnp.square(h), axis=-1, keepdims=True) + HEAD_NORM_EPS)
    return hn.reshape(h.shape[0], h.shape[1], -1) * gain.astype(jnp.float32)


def causal_depthwise_conv(x, w):
    c = x.shape[-1]
    return lax.conv_general_dilated(x, w[:, None, :].astype(x.dtype), window_strides=(1,), padding=[(w.shape[0] - 1, 0)], dimension_numbers=('NWC', 'WIO', 'NWC'), feature_group_count=c)


def partial_rotary(x, positions, rot_dim):
    half = rot_dim // 2
    inv_freq = 1.0 / (ROPE_THETA ** (jnp.arange(half, dtype=jnp.float32) / half))
    ang = positions.astype(jnp.float32)[:, :, None, None] * inv_freq
    cos, sin = jnp.cos(ang), jnp.sin(ang)
    xr = x[..., :rot_dim].astype(jnp.float32)
    x1, x2 = xr[..., :half], xr[..., half:]
    rot = jnp.concatenate([x1 * cos - x2 * sin, x2 * cos + x1 * sin], axis=-1)
    return jnp.concatenate([rot.astype(x.dtype), x[..., rot_dim:]], axis=-1)


def causal_chunk_mask():
    pos = jnp.arange(CHUNK)
    return pos[:, None] >= pos[None, :]


def mlstm_chunked(q, k, v, i_pre, f_pre):
    f32 = jnp.float32
    bsz, t, h, dk = q.shape
    dv = v.shape[-1]
    nc = t // CHUNK
    q = q.astype(f32).reshape(bsz, nc, CHUNK, h, dk)
    k = (k.astype(f32) * dk ** -0.5).reshape(bsz, nc, CHUNK, h, dk)
    v = v.astype(f32).reshape(bsz, nc, CHUNK, h, dv)
    ig = i_pre.astype(f32).reshape(bsz, nc, CHUNK, h)
    lf = jax.nn.log_sigmoid(f_pre.astype(f32)).reshape(bsz, nc, CHUNK, h)
    b = jnp.cumsum(lf, axis=2)
    b_last = b[:, :, -1]
    a = b_last[:, :, None] - b + ig
    m_loc = jnp.max(a, axis=2)
    wk = jnp.exp(a - m_loc[:, :, None])
    c_loc = jnp.einsum('bnjh,bnjhd,bnjhe->bnhde', wk, k, v)
    n_loc = jnp.einsum('bnjh,bnjhd->bnhd', wk, k)

    def step(carry, inp):
        c, n, m = carry
        g, cl, nl, ml = inp
        m_new = jnp.maximum(g + m, ml)
        s_old = jnp.exp(g + m - m_new)
        s_loc = jnp.exp(ml - m_new)
        c_new = s_old[..., None, None] * c + s_loc[..., None, None] * cl
        n_new = s_old[..., None] * n + s_loc[..., None] * nl
        return (c_new, n_new, m_new), (c, n, m)

    init = (jnp.zeros((bsz, h, dk, dv), f32), jnp.zeros((bsz, h, dk), f32), jnp.zeros((bsz, h), f32))
    xs = (jnp.moveaxis(b_last, 1, 0), jnp.moveaxis(c_loc, 1, 0), jnp.moveaxis(n_loc, 1, 0), jnp.moveaxis(m_loc, 1, 0))
    _, (c_prev, n_prev, m_prev) = lax.scan(step, init, xs)
    c_prev = jnp.moveaxis(c_prev, 0, 1)
    n_prev = jnp.moveaxis(n_prev, 0, 1)
    m_prev = jnp.moveaxis(m_prev, 0, 1)
    d_log = b[:, :, :, None, :] - b[:, :, None, :, :] + ig[:, :, None, :, :]
    d_log = jnp.where(causal_chunk_mask()[None, None, :, :, None], d_log, -jnp.inf)
    inter = b + m_prev[:, :, None]
    m_q = jnp.maximum(inter, jnp.max(d_log, axis=3))
    w_ij = jnp.einsum('bnihd,bnjhd->bnijh', q, k) * jnp.exp(d_log - m_q[:, :, :, None])
    s_int = jnp.exp(inter - m_q)
    num = jnp.einsum('bnijh,bnjhe->bnihe', w_ij, v) + s_int[..., None] * jnp.einsum('bnihd,bnhde->bnihe', q, c_prev)
    den = jnp.sum(w_ij, axis=3) + s_int * jnp.einsum('bnihd,bnhd->bnih', q, n_prev)
    out = num / jnp.maximum(jnp.abs(den), jnp.exp(-m_q))[..., None]
    return out.reshape(bsz, t, h, dv)


def gla_chunked(q, k, v, log_a):
    f32 = jnp.float32
    bsz, t, h, dk = q.shape
    dv = v.shape[-1]
    nc = t // CHUNK
    q = (q.astype(f32) * dk ** -0.5).reshape(bsz, nc, CHUNK, h, dk)
    k = k.astype(f32).reshape(bsz, nc, CHUNK, h, dk)
    v = v.astype(f32).reshape(bsz, nc, CHUNK, h, dv)
    la = jnp.cumsum(log_a.astype(f32).reshape(bsz, nc, CHUNK, h, dk), axis=2)
    la_last = la[:, :, -1]
    q_dec = q * jnp.exp(la)
    k_dec = k * jnp.exp(-la)
    k_end = k * jnp.exp(la_last[:, :, None] - la)
    s_loc = jnp.einsum('bnjhd,bnjhe->bnhde', k_end, v)

    def step(s, inp):
        decay, sl = inp
        return jnp.exp(decay)[..., None] * s + sl, s

    _, s_prev = lax.scan(step, jnp.zeros((bsz, h, dk, dv), f32), (jnp.moveaxis(la_last, 1, 0), jnp.moveaxis(s_loc, 1, 0)))
    s_prev = jnp.moveaxis(s_prev, 0, 1)
    attn = jnp.einsum('bnihd,bnjhd->bnijh', q_dec, k_dec)
    attn = jnp.where(causal_chunk_mask()[None, None, :, :, None], attn, 0.0)
    out = jnp.einsum('bnijh,bnjhe->bnihe', attn, v) + jnp.einsum('bnihd,bnhde->bnihe', q_dec, s_prev)
    return out.reshape(bsz, t, h, dv)


def ab_mixer(x, w_in, conv_w, igate_b, fgate_b, gla_w2, gla_b, mlstm_norm_g, gla_norm_g, w_out):
    f32 = jnp.float32
    bsz, t, _ = x.shape
    mq, mk, mv, mi, mf, mo, gq, gk, gv, glr, gr = split_cols(x @ w_in, EVEN_SPLITS)
    qk = jax.nn.silu(causal_depthwise_conv(jnp.concatenate([mq, mk], axis=-1), conv_w))
    mq, mk = jnp.split(qk, 2, axis=-1)
    hm = mlstm_chunked(mq.reshape(bsz, t, MLSTM_HEADS, MLSTM_DK), mk.reshape(bsz, t, MLSTM_HEADS, MLSTM_DK), mv.reshape(bsz, t, MLSTM_HEADS, MLSTM_DV), mi + igate_b, mf + fgate_b)
    hm = head_rms_norm(hm, mlstm_norm_g) * jax.nn.sigmoid(mo.astype(f32))
    log_a = jax.nn.log_sigmoid((glr @ gla_w2 + gla_b).astype(f32)) / GLA_GATE_TAU
    hg = gla_chunked(gq.reshape(bsz, t, GLA_HEADS, GLA_DK), gk.reshape(bsz, t, GLA_HEADS, GLA_DK), gv.reshape(bsz, t, GLA_HEADS, GLA_DV), log_a.reshape(bsz, t, GLA_HEADS, GLA_DK))
    hg = head_rms_norm(hg, gla_norm_g) * jax.nn.silu(gr.astype(f32))
    return jnp.concatenate([hm, hg], axis=-1).astype(x.dtype) @ w_out


def dsa_mixer(x, positions, w_in, w_out):
    f32 = jnp.float32
    bsz, t, _ = x.shape
    q, k, v, qi, ki, wi = split_cols(x @ w_in, ODD_SPLITS)
    q = partial_rotary(q.reshape(bsz, t, ATTN_HEADS, ATTN_DH), positions, ROPE_DIM).astype(f32)
    k = partial_rotary(k.reshape(bsz, t, ATTN_KV_HEADS, ATTN_DH), positions, ROPE_DIM).astype(f32)
    v = v.reshape(bsz, t, ATTN_KV_HEADS, ATTN_DH).astype(f32)
    qi = partial_rotary(qi.reshape(bsz, t, IDX_HEADS, IDX_DH), positions, IDX_ROPE_DIM).astype(f32)
    ki = partial_rotary(ki.reshape(bsz, t, 1, IDX_DH), positions, IDX_ROPE_DIM)[:, :, 0].astype(f32)
    wi = wi.astype(f32) * (IDX_HEADS ** -0.5 * IDX_DH ** -0.5)
    top_k = min(INDEX_TOPK, t // 4)
    n_blk = t // Q_BLOCK
    key_pos = jnp.arange(t)

    def to_blocks(a):
        return jnp.moveaxis(a.reshape(bsz, n_blk, Q_BLOCK, *a.shape[2:]), 1, 0)

    def block(inp):
        qb, qib, wib, qpos = inp
        idx_logits = jnp.einsum('bqhd,bsd->bqhs', qib, ki)
        score = jnp.einsum('bqhs,bqh->bqs', jax.nn.relu(idx_logits), wib)
        score = jnp.where((key_pos[None, :] <= qpos[:, None])[None], score, -jnp.inf)
        _, sel = lax.top_k(score, top_k)
        valid = sel <= qpos[None, :, None]
        k_sel = jax.vmap(lambda kb, ib: kb[ib])(k, sel)
        v_sel = jax.vmap(lambda vb, ib: vb[ib])(v, sel)
        qg = qb.reshape(bsz, Q_BLOCK, ATTN_KV_HEADS, ATTN_HEADS // ATTN_KV_HEADS, ATTN_DH)
        logits = jnp.einsum('bqgrd,bqkgd->bqgrk', qg, k_sel) * ATTN_DH ** -0.5
        logits = jnp.where(valid[:, :, None, None, :], logits, -jnp.inf)
        p = jax.nn.softmax(logits, axis=-1)
        o = jnp.einsum('bqgrk,bqkgd->bqgrd', p, v_sel)
        return o.reshape(bsz, Q_BLOCK, ATTN_WIDTH)

    out = lax.map(block, (to_blocks(q), to_blocks(qi), to_blocks(wi), key_pos.reshape(n_blk, Q_BLOCK)))
    out = jnp.moveaxis(out, 0, 1).reshape(bsz, t, ATTN_WIDTH)
    return out.astype(x.dtype) @ w_out


def moe_ffn(x, router_w, router_bias, w_gate, w_up, w_down):
    f32 = jnp.float32
    aff = jax.nn.sigmoid(jnp.einsum('btd,de->bte', x.astype(f32), router_w.astype(f32)))
    sel = aff + router_bias.astype(f32)
    grp = sel.reshape(sel.shape[0], sel.shape[1], N_GROUPS, EXPERTS_PER_GROUP)
    grp_score = jnp.sum(lax.top_k(grp, GROUP_SCORE_TOPK)[0], axis=-1)
    g = jnp.argmax(grp_score, axis=-1)
    in_grp = jnp.take_along_axis(grp, g[:, :, None, None], axis=2)[:, :, 0]
    _, loc = lax.top_k(in_grp, TOP_K_EXPERTS)
    e_idx = g[..., None] * EXPERTS_PER_GROUP + loc
    gate = jnp.take_along_axis(aff, e_idx, axis=-1)
    gate = gate / jnp.sum(gate, axis=-1, keepdims=True)
    comb = jnp.sum(jax.nn.one_hot(e_idx, N_EXPERTS, dtype=f32) * gate[..., None], axis=-2)
    hg = jnp.einsum('btd,edf->btef', x, w_gate)
    hu = jnp.einsum('btd,edf->btef', x, w_up)
    h = (jax.nn.silu(hg) * hu).astype(f32) * comb[..., None]
    return jnp.einsum('btef,efd->btd', h.astype(x.dtype), w_down)


def setup_inputs(seed: int = 0) -> dict:
    key = jax.random.key(seed)
    ks = jax.random.split(key, 22)
    f32 = jnp.float32

    def nrm(k, shape, scale):
        return jax.random.normal(k, shape, f32) * scale

    x = nrm(ks[0], (BATCH, SEQ, D_MODEL), 1.0)
    positions = jnp.arange(SEQ, dtype=jnp.int32)[None, :] + jax.random.randint(ks[1], (BATCH, 1), 0, MAX_POS_OFFSET, dtype=jnp.int32)
    ab_w_in = nrm(ks[2], (N_EVEN, D_MODEL, EVEN_IN), D_MODEL ** -0.5)
    ab_conv_w = nrm(ks[3], (N_EVEN, CONV_WIDTH, 2 * MLSTM_HEADS * MLSTM_DK), CONV_WIDTH ** -0.5)
    ab_igate_b = nrm(ks[4], (N_EVEN, MLSTM_HEADS), 0.1)
    ab_fgate_b = jnp.linspace(3.0, 6.0, MLSTM_HEADS, dtype=f32)[None, :] + nrm(ks[5], (N_EVEN, MLSTM_HEADS), 0.1)
    ab_gla_gate_w2 = nrm(ks[6], (N_EVEN, GLA_GATE_RANK, GLA_HEADS * GLA_DK), GLA_GATE_RANK ** -0.5)
    ab_gla_gate_b = nrm(ks[7], (N_EVEN, GLA_HEADS * GLA_DK), 0.1)
    ab_mlstm_norm_g = 1.0 + nrm(ks[8], (N_EVEN, MLSTM_HEADS * MLSTM_DV), 0.02)
    ab_gla_norm_g = 1.0 + nrm(ks[9], (N_EVEN, GLA_HEADS * GLA_DV), 0.02)
    ab_w_out = nrm(ks[10], (N_EVEN, MIX_WIDTH, D_MODEL), MIX_WIDTH ** -0.5 * DEEPNORM_BETA)
    c_w_in = nrm(ks[11], (N_ODD, D_MODEL, ODD_IN), D_MODEL ** -0.5)
    c_w_out = nrm(ks[12], (N_ODD, ATTN_WIDTH, D_MODEL), ATTN_WIDTH ** -0.5 * DEEPNORM_BETA)
    router_w = nrm(ks[13], (D_MODEL, N_EXPERTS), D_MODEL ** -0.5)
    router_bias = nrm(ks[14], (N_EXPERTS,), 0.01)
    moe_w_gate = nrm(ks[15], (DEPTH, N_EXPERTS, D_MODEL, EXPERT_FF), D_MODEL ** -0.5)
    moe_w_up = nrm(ks[16], (DEPTH, N_EXPERTS, D_MODEL, EXPERT_FF), D_MODEL ** -0.5)
    moe_w_down = nrm(ks[17], (DEPTH, N_EXPERTS, EXPERT_FF, D_MODEL), EXPERT_FF ** -0.5 * DEEPNORM_BETA)
    ln_mix_g = 1.0 + nrm(ks[18], (DEPTH, D_MODEL), 0.02)
    ln_mix_b = nrm(ks[19], (DEPTH, D_MODEL), 0.02)
    ln_ffn_g = 1.0 + nrm(ks[20], (DEPTH, D_MODEL), 0.02)
    ln_ffn_b = nrm(ks[21], (DEPTH, D_MODEL), 0.02)
    return {'x': x, 'positions': positions, 'ab_w_in': ab_w_in, 'ab_conv_w': ab_conv_w, 'ab_igate_b': ab_igate_b, 'ab_fgate_b': ab_fgate_b, 'ab_gla_gate_w2': ab_gla_gate_w2, 'ab_gla_gate_b': ab_gla_gate_b, 'ab_mlstm_norm_g': ab_mlstm_norm_g, 'ab_gla_norm_g': ab_gla_norm_g, 'ab_w_out': ab_w_out, 'c_w_in': c_w_in, 'c_w_out': c_w_out, 'router_w': router_w, 'router_bias': router_bias, 'moe_w_gate': moe_w_gate, 'moe_w_up': moe_w_up, 'moe_w_down': moe_w_down, 'ln_mix_g': ln_mix_g, 'ln_mix_b': ln_mix_b, 'ln_ffn_g': ln_ffn_g, 'ln_ffn_b': ln_ffn_b}


def reference(x, positions, ab_w_in, ab_conv_w, ab_igate_b, ab_fgate_b, ab_gla_gate_w2, ab_gla_gate_b, ab_mlstm_norm_g, ab_gla_norm_g, ab_w_out, c_w_in, c_w_out, router_w, router_bias, moe_w_gate, moe_w_up, moe_w_down, ln_mix_g, ln_mix_b, ln_ffn_g, ln_ffn_b):
    h = x
    for layer in range(DEPTH):
        j = layer // 2
        if layer % 2 == 0:
            mix = ab_mixer(h, ab_w_in[j], ab_conv_w[j], ab_igate_b[j], ab_fgate_b[j], ab_gla_gate_w2[j], ab_gla_gate_b[j], ab_mlstm_norm_g[j], ab_gla_norm_g[j], ab_w_out[j])
        else:
            mix = dsa_mixer(h, positions, c_w_in[j], c_w_out[j])
        h = layer_norm(DEEPNORM_ALPHA * h + mix, ln_mix_g[layer], ln_mix_b[layer])
        ffn = moe_ffn(h, router_w, router_bias, moe_w_gate[layer], moe_w_up[layer], moe_w_down[layer])
        h = layer_norm(DEEPNORM_ALPHA * h + ffn, ln_ffn_g[layer], ln_ffn_b[layer])
    return h
```

```python
import functools

import jax
import jax.numpy as jnp
import numpy as np
from jax import lax
from jax.experimental import pallas as pl
from jax.experimental.pallas import tpu as pltpu

F32 = jnp.float32
BF16 = jnp.bfloat16
HIGHEST = lax.Precision.HIGHEST

D_MODEL = 2048
DEPTH = 4
MLSTM_HEADS = 4
MLSTM_DK = 128
MLSTM_DV = 256
CONV_WIDTH = 4
GLA_HEADS = 4
GLA_DK = 128
GLA_DV = 256
GLA_GATE_RANK = 16
GLA_GATE_TAU = 16.0
CHUNK = 64
ATTN_HEADS = 16
ATTN_KV_HEADS = 4
ATTN_DH = 128
ROPE_DIM = ATTN_DH // 4
ROPE_THETA = 500000.0
IDX_HEADS = 16
IDX_DH = 64
IDX_ROPE_DIM = IDX_DH // 4
INDEX_TOPK = 256
N_EXPERTS = 16
N_GROUPS = 4
EXPERTS_PER_GROUP = N_EXPERTS // N_GROUPS
EXPERT_FF = 512
DEEPNORM_ALPHA = (2 * DEPTH) ** 0.25
LN_EPS = 1e-5
HEAD_NORM_EPS = 1e-6

LANES = 128
MIX_WIDTH = MLSTM_HEADS * MLSTM_DV + GLA_HEADS * GLA_DV
ATTN_WIDTH = ATTN_HEADS * ATTN_DH
KV_WIDTH = ATTN_KV_HEADS * ATTN_DH
IDX_WIDTH = IDX_HEADS * IDX_DH
Q_PER_KV = ATTN_HEADS // ATTN_KV_HEADS

E_MQ = 0
E_MK = E_MQ + MLSTM_HEADS * MLSTM_DK
E_MV = E_MK + MLSTM_HEADS * MLSTM_DK
E_MO = E_MV + MLSTM_HEADS * MLSTM_DV
E_GQ = E_MO + MLSTM_HEADS * MLSTM_DV
E_GK = E_GQ + GLA_HEADS * GLA_DK
E_GV = E_GK + GLA_HEADS * GLA_DK
E_GR = E_GV + GLA_HEADS * GLA_DV
E_BIG = E_GR + GLA_HEADS * GLA_DV
S_MI = 0
S_MF = S_MI + MLSTM_HEADS
S_GLR = S_MF + MLSTM_HEADS
O_Q = 0
O_K = O_Q + ATTN_WIDTH
O_V = O_K + KV_WIDTH
O_QI = O_V + KV_WIDTH
O_BIG = O_QI + IDX_WIDTH

VMEM_LIMIT = 56 * 1024 * 1024
INT_MIN = -(2**31)
NEG_BIG = -1e30


def _params(sem):
    return pltpu.CompilerParams(dimension_semantics=sem, vmem_limit_bytes=VMEM_LIMIT)


def _mm_kernel(x_ref, w_ref, o_ref):
    o_ref[...] = jnp.dot(x_ref[...], w_ref[...], preferred_element_type=F32).astype(o_ref.dtype)


def _matmul(x, w, out_dtype, tm, tn):
    t, k = x.shape
    n = w.shape[1]
    return pl.pallas_call(
        _mm_kernel,
        grid=(t // tm, n // tn),
        in_specs=[pl.BlockSpec((tm, k), lambda i, j: (i, 0)), pl.BlockSpec((k, tn), lambda i, j: (0, j))],
        out_specs=pl.BlockSpec((tm, tn), lambda i, j: (i, j)),
        out_shape=jax.ShapeDtypeStruct((t, n), out_dtype),
        compiler_params=_params(("parallel", "parallel")),
    )(x, w)


def _layer_norm(z, g, b):
    mu = jnp.mean(z, axis=-1, keepdims=True)
    zc = z - mu
    var = jnp.mean(zc * zc, axis=-1, keepdims=True)
    return zc * lax.rsqrt(var + LN_EPS) * g + b


def _outproj_ln_kernel(mix_ref, w_ref, h_ref, g_ref, b_ref, of_ref, ob_ref):
    y = jnp.dot(mix_ref[...], w_ref[...], preferred_element_type=F32)
    out = _layer_norm(DEEPNORM_ALPHA * h_ref[...] + y, g_ref[...], b_ref[...])
    of_ref[...] = out
    ob_ref[...] = out.astype(BF16)


def _outproj_ln(mix, w, h, g, b, tm=256):
    t, k = mix.shape
    d = w.shape[1]
    row = lambda i: (i, 0)
    fixed = lambda i: (0, 0)
    return pl.pallas_call(
        _outproj_ln_kernel,
        grid=(t // tm,),
        in_specs=[pl.BlockSpec((tm, k), row), pl.BlockSpec((k, d), fixed), pl.BlockSpec((tm, d), row),
                  pl.BlockSpec((1, d), fixed), pl.BlockSpec((1, d), fixed)],
        out_specs=[pl.BlockSpec((tm, d), row), pl.BlockSpec((tm, d), row)],
        out_shape=[jax.ShapeDtypeStruct((t, d), F32), jax.ShapeDtypeStruct((t, d), BF16)],
        compiler_params=_params(("parallel",)),
    )(mix, w, h, g.reshape(1, d), b.reshape(1, d))


def _router_kernel(h_ref, rwt_ref, bias_ref, idx_ref, gate_ref, comb_ref):
    logits = lax.dot_general(rwt_ref[...], h_ref[...], (((1,), (1,)), ((), ())),
                             precision=HIGHEST, preferred_element_type=F32)
    aff = jax.nn.sigmoid(logits)
    sel = aff + bias_ref[...]
    srow = [sel[e:e + 1, :] for e in range(N_EXPERTS)]
    arow = [aff[e:e + 1, :] for e in range(N_EXPERTS)]

    def top2_sum(v):
        best = v[0] + v[1]
        for a in range(len(v)):
            for b in range(a + 1, len(v)):
                if (a, b) != (0, 1):
                    best = jnp.maximum(best, v[a] + v[b])
        return best

    gscore = [top2_sum(srow[g * EXPERTS_PER_GROUP:(g + 1) * EXPERTS_PER_GROUP]) for g in range(N_GROUPS)]
    best = gscore[0]
    gi = jnp.zeros_like(best, dtype=jnp.int32)
    for g in range(1, N_GROUPS):
        upd = gscore[g] > best
        gi = jnp.where(upd, g, gi)
        best = jnp.where(upd, gscore[g], best)

    def pick(rows, k):
        out = rows[k]
        for g in range(1, N_GROUPS):
            out = jnp.where(gi == g, rows[g * EXPERTS_PER_GROUP + k], out)
        return out

    v = [pick(srow, k) for k in range(EXPERTS_PER_GROUP)]
    a = [pick(arow, k) for k in range(EXPERTS_PER_GROUP)]

    def first_argmax(vals):
        m = vals[0]
        for x in vals[1:]:
            m = jnp.maximum(m, x)
        loc = jnp.full_like(gi, len(vals) - 1)
        for k in range(len(vals) - 2, -1, -1):
            loc = jnp.where(vals[k] == m, k, loc)
        return loc

    loc1 = first_argmax(v)
    v2 = [jnp.where(loc1 == k, -jnp.inf, v[k]) for k in range(EXPERTS_PER_GROUP)]
    loc2 = first_argmax(v2)

    def take(vals, loc):
        out = vals[0]
        for k in range(1, len(vals)):
            out = jnp.where(loc == k, vals[k], out)
        return out

    g1 = take(a, loc1)
    g2 = take(a, loc2)
    tot = g1 + g2
    g1 = g1 / tot
    g2 = g2 / tot
    e1 = gi * EXPERTS_PER_GROUP + loc1
    e2 = gi * EXPERTS_PER_GROUP + loc2
    idx_ref[0:1, :] = e1
    idx_ref[1:2, :] = e2
    gate_ref[0:1, :] = g1
    gate_ref[1:2, :] = g2
    erow = lax.broadcasted_iota(jnp.int32, logits.shape, 0)
    comb_ref[...] = jnp.where(erow == e1, g1, 0.0) + jnp.where(erow == e2, g2, 0.0)


def _router(h, rwt, bias, tm=512):
    t, d = h.shape
    return pl.pallas_call(
        _router_kernel,
        grid=(t // tm,),
        in_specs=[pl.BlockSpec((tm, d), lambda i: (i, 0)), pl.BlockSpec((N_EXPERTS, d), lambda i: (0, 0)),
                  pl.BlockSpec((N_EXPERTS, 1), lambda i: (0, 0))],
        out_specs=[pl.BlockSpec((2, tm), lambda i: (0, i)), pl.BlockSpec((2, tm), lambda i: (0, i)),
                   pl.BlockSpec((N_EXPERTS, tm), lambda i: (0, i))],
        out_shape=[jax.ShapeDtypeStruct((2, t), jnp.int32), jax.ShapeDtypeStruct((2, t), F32),
                   jax.ShapeDtypeStruct((N_EXPERTS, t), F32)],
        compiler_params=_params(("parallel",)),
    )(h, rwt, bias)


def _moe_kernel(xb_ref, comb_ref, wg_ref, wu_ref, wd_ref, h_ref, g_ref, b_ref, of_ref, ob_ref, acc_ref):
    e = pl.program_id(1)

    @pl.when(e == 0)
    def _():
        acc_ref[...] = jnp.zeros_like(acc_ref)

    comb = comb_ref[...]
    lane = lax.broadcasted_iota(jnp.int32, comb.shape, 1)
    c = jnp.sum(jnp.where(lane == e, comb, 0.0), axis=1, keepdims=True)

    x = xb_ref[...]
    hg = jnp.dot(x, wg_ref[0], preferred_element_type=F32)
    hu = jnp.dot(x, wu_ref[0], preferred_element_type=F32)
    hh = (hg * jax.nn.sigmoid(hg) * hu) * c
    acc_ref[...] += jnp.dot(hh.astype(BF16), wd_ref[0], preferred_element_type=F32)

    @pl.when(e == pl.num_programs(1) - 1)
    def _():
        out = _layer_norm(DEEPNORM_ALPHA * h_ref[...] + acc_ref[...], g_ref[...], b_ref[...])
        of_ref[...] = out
        ob_ref[...] = out.astype(BF16)


def _moe(xb, comb, wg, wu, wd, h, g, b, tm=512):
    t, d = xb.shape
    ne, _, f = wg.shape
    row = lambda i, e: (i, 0)
    fixed = lambda i, e: (0, 0)
    return pl.pallas_call(
        _moe_kernel,
        grid=(t // tm, ne),
        in_specs=[pl.BlockSpec((tm, d), row), pl.BlockSpec((tm, ne), row),
                  pl.BlockSpec((1, d, f), lambda i, e: (e, 0, 0)), pl.BlockSpec((1, d, f), lambda i, e: (e, 0, 0)),
                  pl.BlockSpec((1, f, d), lambda i, e: (e, 0, 0)),
                  pl.BlockSpec((tm, d), row), pl.BlockSpec((1, d), fixed), pl.BlockSpec((1, d), fixed)],
        out_specs=[pl.BlockSpec((tm, d), row), pl.BlockSpec((tm, d), row)],
        out_shape=[jax.ShapeDtypeStruct((t, d), F32), jax.ShapeDtypeStruct((t, d), BF16)],
        scratch_shapes=[pltpu.VMEM((tm, d), F32)],
        compiler_params=_params(("parallel", "arbitrary")),
    )(xb, comb, wg, wu, wd, h, g.reshape(1, d), b.reshape(1, d))


def _log_sigmoid(x):
    return jnp.minimum(x, 0.0) - jnp.log(1.0 + jnp.exp(-jnp.abs(x)))


def _dot_nt(a, b):
    return lax.dot_general(a, b, (((1,), (1,)), ((), ())), preferred_element_type=F32)


def _dot_tn(a, b, precision=None):
    return lax.dot_general(a, b, (((0,), (0,)), ((), ())), precision=precision, preferred_element_type=F32)


def _head_rms(x, gain):
    return x * lax.rsqrt(jnp.mean(x * x, axis=1, keepdims=True) + HEAD_NORM_EPS) * gain


def _mixer_kernel(big_ref, small_ref, convw_ref, gbias_ref, w2_ref, glab_ref, mng_ref, gng_ref, o_ref,
                  ext_s, qk_s, c_s, n_s, m_s, s_s, *, tb):
    i = pl.program_id(0)
    nqk = 2 * MLSTM_HEADS * MLSTM_DK
    tail = CONV_WIDTH - 1

    @pl.when(i == 0)
    def _():
        ext_s[0:8, :] = jnp.zeros((8, nqk), F32)
        c_s[...] = jnp.zeros_like(c_s)
        n_s[...] = jnp.zeros_like(n_s)
        m_s[...] = jnp.zeros_like(m_s)
        s_s[...] = jnp.zeros_like(s_s)

    ext_s[8:8 + tb, :] = big_ref[:, E_MQ:E_MQ + nqk].astype(F32)
    conv = ext_s[8:8 + tb, :] * convw_ref[tail:tail + 1, :]
    for j in range(tail):
        conv = conv + ext_s[8 - tail + j:8 - tail + j + tb, :] * convw_ref[j:j + 1, :]
    qk_s[...] = conv * jax.nn.sigmoid(conv)
    ext_s[0:8, :] = ext_s[tb:tb + 8, :]

    rows = lax.broadcasted_iota(jnp.int32, (CHUNK, CHUNK), 0)
    cols = lax.broadcasted_iota(jnp.int32, (CHUNK, CHUNK), 1)
    causal = rows >= cols
    ltri = causal.astype(F32)
    eye = (rows == cols).astype(F32)
    lane = lax.broadcasted_iota(jnp.int32, (CHUNK, LANES), 1)
    is_f = (lane >= S_MF) & (lane < S_MF + MLSTM_HEADS)

    def chunk(c, carry):
        r0 = pl.multiple_of(c * CHUNK, CHUNK)
        rs = pl.ds(r0, CHUNK)
        gates = small_ref[rs, :]
        pre = gates + gbias_ref[...]
        xg = jnp.where(is_f, _log_sigmoid(pre), pre)
        cum = jnp.dot(ltri, xg, precision=HIGHEST, preferred_element_type=F32)
        y = jnp.where(is_f, cum, xg)
        yt = _dot_tn(y, eye, precision=HIGHEST)

        for h in range(MLSTM_HEADS):
            q = qk_s[rs, h * MLSTM_DK:(h + 1) * MLSTM_DK]
            k = qk_s[rs, (MLSTM_HEADS + h) * MLSTM_DK:(MLSTM_HEADS + h + 1) * MLSTM_DK] * MLSTM_DK ** -0.5
            v = big_ref[rs, E_MV + h * MLSTM_DV:E_MV + (h + 1) * MLSTM_DV]
            ig_c = y[:, S_MI + h:S_MI + h + 1]
            b_c = y[:, S_MF + h:S_MF + h + 1]
            ig_r = yt[S_MI + h:S_MI + h + 1, :]
            b_r = yt[S_MF + h:S_MF + h + 1, :]
            b_last = y[CHUNK - 1:CHUNK, S_MF + h:S_MF + h + 1]
            c_prev = c_s[h]
            n_prev = n_s[h:h + 1, :]
            m_prev = m_s[h:h + 1, 0:1]
            d_log = jnp.where(causal, b_c - b_r + ig_r, -jnp.inf)
            inter = b_c + m_prev
            m_q = jnp.maximum(inter, jnp.max(d_log, axis=1, keepdims=True))
            qb = q.astype(BF16)
            w = _dot_nt(qb, k.astype(BF16)) * jnp.exp(d_log - m_q)
            s_int = jnp.exp(inter - m_q)
            num = jnp.dot(w.astype(BF16), v, preferred_element_type=F32) + s_int * jnp.dot(
                qb, c_prev.astype(BF16), preferred_element_type=F32)
            den = jnp.sum(w, axis=1, keepdims=True) + s_int * jnp.sum(q * n_prev, axis=1, keepdims=True)
            out = num / jnp.maximum(jnp.abs(den), jnp.exp(-m_q))
            a_c = b_last - b_c + ig_c
            m_loc = jnp.max(a_c, axis=0, keepdims=True)
            kw = k * jnp.exp(a_c - m_loc)
            c_loc = _dot_tn(kw.astype(BF16), v)
            n_loc = jnp.sum(kw, axis=0, keepdims=True)
            m_new = jnp.maximum(b_last + m_prev, m_loc)
            s_old = jnp.exp(b_last + m_prev - m_new)
            s_loc = jnp.exp(m_loc - m_new)
            c_s[h] = s_old * c_prev + s_loc * c_loc
            n_s[h:h + 1, :] = s_old * n_prev + s_loc * n_loc
            m_s[h:h + 1, :] = jnp.broadcast_to(m_new, (1, LANES))
            hn = _head_rms(out, mng_ref[:, h * MLSTM_DV:(h + 1) * MLSTM_DV])
            og = big_ref[rs, E_MO + h * MLSTM_DV:E_MO + (h + 1) * MLSTM_DV].astype(F32)
            o_ref[rs, h * MLSTM_DV:(h + 1) * MLSTM_DV] = (hn * jax.nn.sigmoid(og)).astype(BF16)

        z = jnp.dot(gates, w2_ref[...], precision=HIGHEST, preferred_element_type=F32) + glab_ref[...]
        log_a = _log_sigmoid(z) / GLA_GATE_TAU
        la_all = jnp.dot(ltri, log_a, precision=HIGHEST, preferred_element_type=F32)
        for h in range(GLA_HEADS):
            la = la_all[:, h * GLA_DK:(h + 1) * GLA_DK]
            la_last = la[CHUNK - 1:CHUNK, :]
            q = big_ref[rs, E_GQ + h * GLA_DK:E_GQ + (h + 1) * GLA_DK].astype(F32) * GLA_DK ** -0.5
            k = big_ref[rs, E_GK + h * GLA_DK:E_GK + (h + 1) * GLA_DK].astype(F32)
            v = big_ref[rs, E_GV + h * GLA_DV:E_GV + (h + 1) * GLA_DV]
            q_dec = (q * jnp.exp(la)).astype(BF16)
            k_dec = (k * jnp.exp(-la)).astype(BF16)
            k_end = (k * jnp.exp(la_last - la)).astype(BF16)
            attn = jnp.where(causal, _dot_nt(q_dec, k_dec), 0.0)
            st_prev = s_s[h]
            out = jnp.dot(attn.astype(BF16), v, preferred_element_type=F32) + _dot_nt(q_dec, st_prev.astype(BF16))
            s_s[h] = jnp.exp(la_last) * st_prev + _dot_tn(v, k_end)
            hn = _head_rms(out, gng_ref[:, h * GLA_DV:(h + 1) * GLA_DV])
            gr = big_ref[rs, E_GR + h * GLA_DV:E_GR + (h + 1) * GLA_DV].astype(F32)
            col = MLSTM_HEADS * MLSTM_DV + h * GLA_DV
            o_ref[rs, col:col + GLA_DV] = (hn * (gr * jax.nn.sigmoid(gr))).astype(BF16)
        return carry

    lax.fori_loop(0, tb // CHUNK, chunk, 0)


def _mixer(big, small, convw, gbias, w2pad, glab, mng, gng, tb=512):
    t = big.shape[0]
    nqk = 2 * MLSTM_HEADS * MLSTM_DK
    fixed = lambda i: (0, 0)
    return pl.pallas_call(
        functools.partial(_mixer_kernel, tb=tb),
        grid=(t // tb,),
        in_specs=[pl.BlockSpec((tb, E_BIG), lambda i: (i, 0)), pl.BlockSpec((tb, LANES), lambda i: (i, 0)),
                  pl.BlockSpec((CONV_WIDTH, nqk), fixed), pl.BlockSpec((1, LANES), fixed),
                  pl.BlockSpec((LANES, GLA_HEADS * GLA_DK), fixed), pl.BlockSpec((1, GLA_HEADS * GLA_DK), fixed),
                  pl.BlockSpec((1, MLSTM_HEADS * MLSTM_DV), fixed), pl.BlockSpec((1, GLA_HEADS * GLA_DV), fixed)],
        out_specs=pl.BlockSpec((tb, MIX_WIDTH), lambda i: (i, 0)),
        out_shape=jax.ShapeDtypeStruct((t, MIX_WIDTH), BF16),
        scratch_shapes=[pltpu.VMEM((tb + 8, nqk), F32), pltpu.VMEM((tb, nqk), F32),
                        pltpu.VMEM((MLSTM_HEADS, MLSTM_DK, MLSTM_DV), F32), pltpu.VMEM((8, MLSTM_DK), F32),
                        pltpu.VMEM((8, LANES), F32), pltpu.VMEM((GLA_HEADS, GLA_DV, GLA_DK), F32)],
        compiler_params=_params(("arbitrary",)),
    )(big, small, convw, gbias, w2pad, glab, mng, gng)


def _rope_kernel(big_ref, small_ref, pos_ref, freq_ref, sign_ref, q_ref, k_ref, qi_ref, ki_ref, wi_ref):
    pos = pos_ref[...].astype(F32)
    lane = lax.broadcasted_iota(jnp.int32, (pos.shape[0], LANES), 1)

    def tables(row):
        ang = pos * freq_ref[row:row + 1, :]
        return jnp.cos(ang), jnp.sin(ang) * sign_ref[row:row + 1, :]

    def rotate(x, cos, sin, half, period):
        up = pltpu.roll(x, LANES - half, axis=1)
        dn = pltpu.roll(x, half, axis=1)
        return x * cos + jnp.where((lane % period) < half, up, dn) * sin

    cos_a, sin_a = tables(0)
    for h in range(ATTN_HEADS):
        x = big_ref[:, O_Q + h * ATTN_DH:O_Q + (h + 1) * ATTN_DH].astype(F32)
        r = rotate(x, cos_a, sin_a, ROPE_DIM // 2, ATTN_DH) * ATTN_DH ** -0.5
        q_ref[:, h * ATTN_DH:(h + 1) * ATTN_DH] = r.astype(BF16)
    for h in range(ATTN_KV_HEADS):
        x = big_ref[:, O_K + h * ATTN_DH:O_K + (h + 1) * ATTN_DH].astype(F32)
        k_ref[:, h * ATTN_DH:(h + 1) * ATTN_DH] = rotate(x, cos_a, sin_a, ROPE_DIM // 2, ATTN_DH).astype(BF16)
    cos_i, sin_i = tables(1)
    for h in range(IDX_WIDTH // LANES):
        x = big_ref[:, O_QI + h * LANES:O_QI + (h + 1) * LANES].astype(F32)
        qi_ref[:, h * LANES:(h + 1) * LANES] = rotate(x, cos_i, sin_i, IDX_ROPE_DIM // 2, IDX_DH).astype(BF16)
    cos_k, sin_k = tables(2)
    sm = small_ref[...]
    ki_ref[...] = rotate(sm, cos_k, sin_k, IDX_ROPE_DIM // 2, IDX_DH)[:, :IDX_DH].astype(BF16)
    wi_ref[...] = pltpu.roll(sm, LANES - IDX_DH, axis=1) * (IDX_HEADS ** -0.5 * IDX_DH ** -0.5)


def _rope_tables():
    def pattern(rot_dim, period, width):
        half = rot_dim // 2
        inv = 1.0 / (ROPE_THETA ** (jnp.arange(half, dtype=F32) / half))
        l = np.arange(LANES)
        inside = ((l % period) < rot_dim) & (l < width)
        freq = jnp.where(inside, inv[(l % period) % half], 0.0)
        sign = np.where(inside, np.where((l % period) < half, -1.0, 1.0), 0.0)
        return freq, jnp.asarray(sign, F32)

    fa, sa = pattern(ROPE_DIM, ATTN_DH, LANES)
    fi, si = pattern(IDX_ROPE_DIM, IDX_DH, LANES)
    fk, sk = pattern(IDX_ROPE_DIM, IDX_DH, IDX_DH)
    z = jnp.zeros((5, LANES), F32)
    return jnp.concatenate([jnp.stack([fa, fi, fk]), z]), jnp.concatenate([jnp.stack([sa, si, sk]), z])


def _rope(big, small, pos, tm=512):
    t = big.shape[0]
    freq, sign = _rope_tables()
    row = lambda i: (i, 0)
    fixed = lambda i: (0, 0)
    return pl.pallas_call(
        _rope_kernel,
        grid=(t // tm,),
        in_specs=[pl.BlockSpec((tm, O_BIG), row), pl.BlockSpec((tm, LANES), row), pl.BlockSpec((tm, 1), row),
                  pl.BlockSpec((8, LANES), fixed), pl.BlockSpec((8, LANES), fixed)],
        out_specs=[pl.BlockSpec((tm, ATTN_WIDTH), row), pl.BlockSpec((tm, KV_WIDTH), row),
                   pl.BlockSpec((tm, IDX_WIDTH), row), pl.BlockSpec((tm, IDX_DH), row),
                   pl.BlockSpec((tm, LANES), row)],
        out_shape=[jax.ShapeDtypeStruct((t, ATTN_WIDTH), BF16), jax.ShapeDtypeStruct((t, KV_WIDTH), BF16),
                   jax.ShapeDtypeStruct((t, IDX_WIDTH), BF16), jax.ShapeDtypeStruct((t, IDX_DH), BF16),
                   jax.ShapeDtypeStruct((t, LANES), F32)],
        compiler_params=_params(("parallel",)),
    )(big, small, pos, freq, sign)


def _dsa_kernel(qtab_ref, ktab_ref, q_ref, qi_ref, wi_ref, ki_ref, k_ref, v_ref, o_ref,
                keys_s, thr_s, m_s, l_s, acc_s, *, tq, tk, top_k):
    n = pl.program_id(0)
    qb = qtab_ref[n]
    kb = ktab_ref[n]
    last_kb = (qb * tq + tq - 1) // tk
    n_tiles = last_kb + 1
    rq = Q_PER_KV * tq

    @pl.when(kb == 0)
    def _():
        w = wi_ref[...]
        qpos = qb * tq + lax.broadcasted_iota(jnp.int32, (tq, tk), 0)
        kiota = lax.broadcasted_iota(jnp.int32, (tq, tk), 1)

        def score_tile(c, carry):
            c0 = pl.multiple_of(c * tk, tk)
            kt = ki_ref[pl.ds(c0, tk), :]
            sc = jnp.zeros((tq, tk), F32)
            for h in range(IDX_HEADS):
                lg = _dot_nt(qi_ref[:, h * IDX_DH:(h + 1) * IDX_DH], kt)
                sc = sc + jnp.maximum(lg, 0.0) * w[:, h:h + 1]
            bits = pltpu.bitcast(sc, jnp.int32)
            key = bits ^ ((bits >> 31) & 0x7FFFFFFF)
            keys_s[:, pl.ds(c0, tk)] = jnp.where(kiota + c0 <= qpos, key, INT_MIN)
            return carry

        lax.fori_loop(0, n_tiles, score_tile, 0)

        def count_ge(cand):
            def body(c, acc):
                c0 = pl.multiple_of(c * tk, tk)
                ge = (keys_s[:, pl.ds(c0, tk)] >= cand).astype(jnp.int32)
                for u in range(tk // LANES):
                    acc = acc + ge[:, u * LANES:(u + 1) * LANES]
                return acc

            acc = lax.fori_loop(0, n_tiles, body, jnp.zeros((tq, LANES), jnp.int32))
            return jnp.sum(acc, axis=1, keepdims=True)

        zero = jnp.zeros((tq, 1), jnp.int32)
        base = jnp.where(count_ge(zero) >= top_k, zero, INT_MIN)

        def bit_step(b, base):
            cand = base | jnp.left_shift(jnp.int32(1), 30 - b)
            return jnp.where(count_ge(cand) >= top_k, cand, base)

        thr = lax.fori_loop(0, 31, bit_step, base)
        thr_s[...] = jnp.maximum(thr, INT_MIN + 1)
        m_s[...] = jnp.full_like(m_s, NEG_BIG)
        l_s[...] = jnp.zeros_like(l_s)
        acc_s[...] = jnp.zeros_like(acc_s)

    c0 = pl.multiple_of(kb * tk, tk)
    keyt = jnp.concatenate([keys_s[:, pl.ds(c0, tk)]] * Q_PER_KV, axis=0)
    mask = keyt >= jnp.concatenate([thr_s[...]] * Q_PER_KV, axis=0)
    for g in range(ATTN_KV_HEADS):
        qg = jnp.concatenate(
            [q_ref[:, (g * Q_PER_KV + r) * ATTN_DH:(g * Q_PER_KV + r + 1) * ATTN_DH] for r in range(Q_PER_KV)], axis=0)
        s = _dot_nt(qg, k_ref[:, g * ATTN_DH:(g + 1) * ATTN_DH])
        s = jnp.where(mask, s, NEG_BIG)
        m_prev = m_s[g]
        m_new = jnp.maximum(m_prev, jnp.max(s, axis=1, keepdims=True))
        p = jnp.where(mask, jnp.exp(s - m_new), 0.0)
        alpha = jnp.exp(m_prev - m_new)
        l_s[g] = alpha * l_s[g] + jnp.sum(p, axis=1, keepdims=True)
        acc_s[g] = alpha * acc_s[g] + jnp.dot(p.astype(BF16), v_ref[:, g * ATTN_DH:(g + 1) * ATTN_DH],
                                              preferred_element_type=F32)
        m_s[g] = m_new

    @pl.when(kb == last_kb)
    def _():
        for g in range(ATTN_KV_HEADS):
            res = acc_s[g] / l_s[g]
            for r in range(Q_PER_KV):
                hcol = (g * Q_PER_KV + r) * ATTN_DH
                o_ref[:, hcol:hcol + ATTN_DH] = res[r * tq:(r + 1) * tq, :].astype(BF16)


def _dsa(q, qi, wi, ki, k, big, tq=128, tk=512):
    t = q.shape[0]
    top_k = min(INDEX_TOPK, t // 4)
    nq = t // tq
    pairs = [(i, j) for i in range(nq) for j in range((i * tq + tq - 1) // tk + 1)]
    qtab = jnp.asarray([p[0] for p in pairs], jnp.int32)
    ktab = jnp.asarray([p[1] for p in pairs], jnp.int32)
    v_blk = O_V // KV_WIDTH
    qrow = lambda n, qt, kt: (qt[n], 0)
    grid_spec = pltpu.PrefetchScalarGridSpec(
        num_scalar_prefetch=2,
        grid=(len(pairs),),
        in_specs=[pl.BlockSpec((tq, ATTN_WIDTH), qrow), pl.BlockSpec((tq, IDX_WIDTH), qrow),
                  pl.BlockSpec((tq, LANES), qrow), pl.BlockSpec((t, IDX_DH), lambda n, qt, kt: (0, 0)),
                  pl.BlockSpec((tk, KV_WIDTH), lambda n, qt, kt: (kt[n], 0)),
                  pl.BlockSpec((tk, KV_WIDTH), lambda n, qt, kt: (kt[n], v_blk))],
        out_specs=pl.BlockSpec((tq, ATTN_WIDTH), qrow),
        scratch_shapes=[pltpu.VMEM((tq, t), jnp.int32), pltpu.VMEM((tq, 1), jnp.int32),
                        pltpu.VMEM((ATTN_KV_HEADS, Q_PER_KV * tq, 1), F32),
                        pltpu.VMEM((ATTN_KV_HEADS, Q_PER_KV * tq, 1), F32),
                        pltpu.VMEM((ATTN_KV_HEADS, Q_PER_KV * tq, ATTN_DH), F32)],
    )
    return pl.pallas_call(
        functools.partial(_dsa_kernel, tq=tq, tk=tk, top_k=top_k),
        grid_spec=grid_spec,
        out_shape=jax.ShapeDtypeStruct((t, ATTN_WIDTH), BF16),
        compiler_params=_params(("arbitrary",)),
    )(qtab, ktab, q, qi, wi, ki, k, big)


def _pad_cols(w, width):
    return jnp.pad(w, ((0, 0), (0, width - w.shape[1])))


def _even_weights(w_in, igate_b, fgate_b, gla_w2, gla_b):
    cuts = np.cumsum([512, 512, 1024, 4, 4, 1024, 512, 512, 1024, 16, 1024])[:-1]
    mq, mk, mv, mi, mf, mo, gq, gk, gv, glr, gr = jnp.split(w_in, [int(c) for c in cuts], axis=1)
    w_big = jnp.concatenate([mq, mk, mv, mo, gq, gk, gv, gr], axis=1).astype(BF16)
    w_small = _pad_cols(jnp.concatenate([mi, mf, glr], axis=1), LANES).astype(BF16)
    gbias = _pad_cols(jnp.concatenate([igate_b, fgate_b])[None, :], LANES)
    w2pad = jnp.zeros((LANES, GLA_HEADS * GLA_DK), F32).at[S_GLR:S_GLR + GLA_GATE_RANK].set(gla_w2)
    return w_big, w_small, gbias, w2pad, gla_b[None, :]


def _odd_weights(w_in):
    cuts = np.cumsum([ATTN_WIDTH, KV_WIDTH, KV_WIDTH, IDX_WIDTH, IDX_DH, IDX_HEADS])[:-1]
    q, k, v, qi, ki, wi = jnp.split(w_in, [int(c) for c in cuts], axis=1)
    w_big = jnp.concatenate([q, k, v, qi], axis=1).astype(BF16)
    w_small = _pad_cols(jnp.concatenate([ki, wi], axis=1), LANES).astype(BF16)
    return w_big, w_small


def _even_mixer(hb, w_in, conv_w, igate_b, fgate_b, gla_w2, gla_b, mng, gng):
    w_big, w_small, gbias, w2pad, glab = _even_weights(w_in, igate_b, fgate_b, gla_w2, gla_b)
    big = _matmul(hb, w_big, BF16, 1024, 1024)
    small = _matmul(hb, w_small, F32, 1024, LANES)
    return _mixer(big, small, conv_w, gbias, w2pad, glab, mng[None, :], gng[None, :])


def _odd_mixer(hb, pos, w_in):
    w_big, w_small = _odd_weights(w_in)
    big = _matmul(hb, w_big, BF16, 1024, 1024)
    small = _matmul(hb, w_small, F32, 1024, LANES)
    q, k, qi, ki, wi = _rope(big, small, pos)
    return _dsa(q, qi, wi, ki, k, big)


def kernel(x, positions, ab_w_in, ab_conv_w, ab_igate_b, ab_fgate_b, ab_gla_gate_w2, ab_gla_gate_b, ab_mlstm_norm_g, ab_gla_norm_g, ab_w_out, c_w_in, c_w_out, router_w, router_bias, moe_w_gate, moe_w_up, moe_w_down, ln_mix_g, ln_mix_b, ln_ffn_g, ln_ffn_b):
    bsz, t, d = x.shape
    outs = []
    rwt = router_w.T
    rbias = router_bias[:, None]
    for bi in range(bsz):
        h = x[bi]
        hb = h.astype(BF16)
        pos = positions[bi][:, None]
        for layer in range(DEPTH):
            j = layer // 2
            if layer % 2 == 0:
                mix = _even_mixer(hb, ab_w_in[j], ab_conv_w[j], ab_igate_b[j], ab_fgate_b[j], ab_gla_gate_w2[j],
                                  ab_gla_gate_b[j], ab_mlstm_norm_g[j], ab_gla_norm_g[j])
                w_out = ab_w_out[j]
            else:
                mix = _odd_mixer(hb, pos, c_w_in[j])
                w_out = c_w_out[j]
            h, hb = _outproj_ln(mix, w_out.astype(BF16), h, ln_mix_g[layer], ln_mix_b[layer])
            _, _, comb_t = _router(h, rwt, rbias)
            h, hb = _moe(hb, comb_t.T, moe_w_gate[layer].astype(BF16), moe_w_up[layer].astype(BF16),
                         moe_w_down[layer].astype(BF16), h, ln_ffn_g[layer], ln_ffn_b[layer])
        outs.append(h)
    return jnp.stack(outs)
```

```python
import functools

import jax
import jax.numpy as jnp
import numpy as np
from jax import lax
from jax.experimental import pallas as pl
from jax.experimental.pallas import tpu as pltpu

F32 = jnp.float32
BF16 = jnp.bfloat16
HIGHEST = lax.Precision.HIGHEST

D_MODEL = 2048
DEPTH = 4
MLSTM_HEADS = 4
MLSTM_DK = 128
MLSTM_DV = 256
CONV_WIDTH = 4
GLA_HEADS = 4
GLA_DK = 128
GLA_DV = 256
GLA_GATE_RANK = 16
GLA_GATE_TAU = 16.0
CHUNK = 64
ATTN_HEADS = 16
ATTN_KV_HEADS = 4
ATTN_DH = 128
ROPE_DIM = ATTN_DH // 4
ROPE_THETA = 500000.0
IDX_HEADS = 16
IDX_DH = 64
IDX_ROPE_DIM = IDX_DH // 4
INDEX_TOPK = 256
N_EXPERTS = 16
N_GROUPS = 4
EXPERTS_PER_GROUP = N_EXPERTS // N_GROUPS
EXPERT_FF = 512
DEEPNORM_ALPHA = (2 * DEPTH) ** 0.25
LN_EPS = 1e-5
HEAD_NORM_EPS = 1e-6

LANES = 128
MIX_WIDTH = MLSTM_HEADS * MLSTM_DV + GLA_HEADS * GLA_DV
ATTN_WIDTH = ATTN_HEADS * ATTN_DH
KV_WIDTH = ATTN_KV_HEADS * ATTN_DH
IDX_WIDTH = IDX_HEADS * IDX_DH
Q_PER_KV = ATTN_HEADS // ATTN_KV_HEADS

E_MQ = 0
E_MK = E_MQ + MLSTM_HEADS * MLSTM_DK
E_MV = E_MK + MLSTM_HEADS * MLSTM_DK
E_MO = E_MV + MLSTM_HEADS * MLSTM_DV
E_GQ = E_MO + MLSTM_HEADS * MLSTM_DV
E_GK = E_GQ + GLA_HEADS * GLA_DK
E_GV = E_GK + GLA_HEADS * GLA_DK
E_GR = E_GV + GLA_HEADS * GLA_DV
E_BIG = E_GR + GLA_HEADS * GLA_DV
S_MI = 0
S_MF = S_MI + MLSTM_HEADS
S_GLR = S_MF + MLSTM_HEADS
O_Q = 0
O_K = O_Q + ATTN_WIDTH
O_V = O_K + KV_WIDTH
O_QI = O_V + KV_WIDTH
O_BIG = O_QI + IDX_WIDTH

VMEM_LIMIT = 56 * 1024 * 1024
DSA_VMEM_LIMIT = 60 * 1024 * 1024
QK_SCALE_LOG2 = ATTN_DH ** -0.5 * 1.4426950408889634
INT_MIN = -(2**31)
NEG_BIG = -1e30


def _params(sem):
    return pltpu.CompilerParams(dimension_semantics=sem, vmem_limit_bytes=VMEM_LIMIT)


def _mm_kernel(x_ref, w_ref, o_ref):
    o_ref[...] = jnp.dot(x_ref[...], w_ref[...], preferred_element_type=F32).astype(o_ref.dtype)


def _matmul(x, w, out_dtype, tm, tn):
    t, k = x.shape
    n = w.shape[1]
    return pl.pallas_call(
        _mm_kernel,
        grid=(t // tm, n // tn),
        in_specs=[pl.BlockSpec((tm, k), lambda i, j: (i, 0)), pl.BlockSpec((k, tn), lambda i, j: (0, j))],
        out_specs=pl.BlockSpec((tm, tn), lambda i, j: (i, j)),
        out_shape=jax.ShapeDtypeStruct((t, n), out_dtype),
        compiler_params=_params(("parallel", "parallel")),
    )(x, w)


def _layer_norm(z, g, b):
    mu = jnp.mean(z, axis=-1, keepdims=True)
    zc = z - mu
    var = jnp.mean(zc * zc, axis=-1, keepdims=True)
    return zc * lax.rsqrt(var + LN_EPS) * g + b


def _outproj_ln_kernel(mix_ref, w_ref, h_ref, g_ref, b_ref, of_ref, ob_ref):
    y = jnp.dot(mix_ref[...], w_ref[...], preferred_element_type=F32)
    out = _layer_norm(DEEPNORM_ALPHA * h_ref[...] + y, g_ref[...], b_ref[...])
    of_ref[...] = out
    ob_ref[...] = out.astype(BF16)


def _outproj_ln(mix, w, h, g, b, tm=256):
    t, k = mix.shape
    d = w.shape[1]
    row = lambda i: (i, 0)
    fixed = lambda i: (0, 0)
    return pl.pallas_call(
        _outproj_ln_kernel,
        grid=(t // tm,),
        in_specs=[pl.BlockSpec((tm, k), row), pl.BlockSpec((k, d), fixed), pl.BlockSpec((tm, d), row),
                  pl.BlockSpec((1, d), fixed), pl.BlockSpec((1, d), fixed)],
        out_specs=[pl.BlockSpec((tm, d), row), pl.BlockSpec((tm, d), row)],
        out_shape=[jax.ShapeDtypeStruct((t, d), F32), jax.ShapeDtypeStruct((t, d), BF16)],
        compiler_params=_params(("parallel",)),
    )(mix, w, h, g.reshape(1, d), b.reshape(1, d))


def _router_kernel(h_ref, rwt_ref, bias_ref, idx_ref, gate_ref, comb_ref):
    logits = lax.dot_general(rwt_ref[...], h_ref[...], (((1,), (1,)), ((), ())),
                             precision=HIGHEST, preferred_element_type=F32)
    aff = jax.nn.sigmoid(logits)
    sel = aff + bias_ref[...]
    srow = [sel[e:e + 1, :] for e in range(N_EXPERTS)]
    arow = [aff[e:e + 1, :] for e in range(N_EXPERTS)]

    def top2_sum(v):
        best = v[0] + v[1]
        for a in range(len(v)):
            for b in range(a + 1, len(v)):
                if (a, b) != (0, 1):
                    best = jnp.maximum(best, v[a] + v[b])
        return best

    gscore = [top2_sum(srow[g * EXPERTS_PER_GROUP:(g + 1) * EXPERTS_PER_GROUP]) for g in range(N_GROUPS)]
    best = gscore[0]
    gi = jnp.zeros_like(best, dtype=jnp.int32)
    for g in range(1, N_GROUPS):
        upd = gscore[g] > best
        gi = jnp.where(upd, g, gi)
        best = jnp.where(upd, gscore[g], best)

    def pick(rows, k):
        out = rows[k]
        for g in range(1, N_GROUPS):
            out = jnp.where(gi == g, rows[g * EXPERTS_PER_GROUP + k], out)
        return out

    v = [pick(srow, k) for k in range(EXPERTS_PER_GROUP)]
    a = [pick(arow, k) for k in range(EXPERTS_PER_GROUP)]

    def first_argmax(vals):
        m = vals[0]
        for x in vals[1:]:
            m = jnp.maximum(m, x)
        loc = jnp.full_like(gi, len(vals) - 1)
        for k in range(len(vals) - 2, -1, -1):
            loc = jnp.where(vals[k] == m, k, loc)
        return loc

    loc1 = first_argmax(v)
    v2 = [jnp.where(loc1 == k, -jnp.inf, v[k]) for k in range(EXPERTS_PER_GROUP)]
    loc2 = first_argmax(v2)

    def take(vals, loc):
        out = vals[0]
        for k in range(1, len(vals)):
            out = jnp.where(loc == k, vals[k], out)
        return out

    g1 = take(a, loc1)
    g2 = take(a, loc2)
    tot = g1 + g2
    g1 = g1 / tot
    g2 = g2 / tot
    e1 = gi * EXPERTS_PER_GROUP + loc1
    e2 = gi * EXPERTS_PER_GROUP + loc2
    idx_ref[0:1, :] = e1
    idx_ref[1:2, :] = e2
    gate_ref[0:1, :] = g1
    gate_ref[1:2, :] = g2
    erow = lax.broadcasted_iota(jnp.int32, logits.shape, 0)
    comb_ref[...] = jnp.where(erow == e1, g1, 0.0) + jnp.where(erow == e2, g2, 0.0)


def _router(h, rwt, bias, tm=512):
    t, d = h.shape
    return pl.pallas_call(
        _router_kernel,
        grid=(t // tm,),
        in_specs=[pl.BlockSpec((tm, d), lambda i: (i, 0)), pl.BlockSpec((N_EXPERTS, d), lambda i: (0, 0)),
                  pl.BlockSpec((N_EXPERTS, 1), lambda i: (0, 0))],
        out_specs=[pl.BlockSpec((2, tm), lambda i: (0, i)), pl.BlockSpec((2, tm), lambda i: (0, i)),
                   pl.BlockSpec((N_EXPERTS, tm), lambda i: (0, i))],
        out_shape=[jax.ShapeDtypeStruct((2, t), jnp.int32), jax.ShapeDtypeStruct((2, t), F32),
                   jax.ShapeDtypeStruct((N_EXPERTS, t), F32)],
        compiler_params=_params(("parallel",)),
    )(h, rwt, bias)


def _moe_kernel(xb_ref, comb_ref, wg_ref, wu_ref, wd_ref, h_ref, g_ref, b_ref, of_ref, ob_ref, acc_ref):
    e = pl.program_id(1)

    @pl.when(e == 0)
    def _():
        acc_ref[...] = jnp.zeros_like(acc_ref)

    comb = comb_ref[...]
    lane = lax.broadcasted_iota(jnp.int32, comb.shape, 1)
    c = jnp.sum(jnp.where(lane == e, comb, 0.0), axis=1, keepdims=True)

    x = xb_ref[...]
    hg = jnp.dot(x, wg_ref[0], preferred_element_type=F32)
    hu = jnp.dot(x, wu_ref[0], preferred_element_type=F32)
    hh = (hg * jax.nn.sigmoid(hg) * hu) * c
    acc_ref[...] += jnp.dot(hh.astype(BF16), wd_ref[0], preferred_element_type=F32)

    @pl.when(e == pl.num_programs(1) - 1)
    def _():
        out = _layer_norm(DEEPNORM_ALPHA * h_ref[...] + acc_ref[...], g_ref[...], b_ref[...])
        of_ref[...] = out
        ob_ref[...] = out.astype(BF16)


def _moe(xb, comb, wg, wu, wd, h, g, b, tm=512):
    t, d = xb.shape
    ne, _, f = wg.shape
    row = lambda i, e: (i, 0)
    fixed = lambda i, e: (0, 0)
    return pl.pallas_call(
        _moe_kernel,
        grid=(t // tm, ne),
        in_specs=[pl.BlockSpec((tm, d), row), pl.BlockSpec((tm, ne), row),
                  pl.BlockSpec((1, d, f), lambda i, e: (e, 0, 0)), pl.BlockSpec((1, d, f), lambda i, e: (e, 0, 0)),
                  pl.BlockSpec((1, f, d), lambda i, e: (e, 0, 0)),
                  pl.BlockSpec((tm, d), row), pl.BlockSpec((1, d), fixed), pl.BlockSpec((1, d), fixed)],
        out_specs=[pl.BlockSpec((tm, d), row), pl.BlockSpec((tm, d), row)],
        out_shape=[jax.ShapeDtypeStruct((t, d), F32), jax.ShapeDtypeStruct((t, d), BF16)],
        scratch_shapes=[pltpu.VMEM((tm, d), F32)],
        compiler_params=_params(("parallel", "arbitrary")),
    )(xb, comb, wg, wu, wd, h, g.reshape(1, d), b.reshape(1, d))


def _log_sigmoid(x):
    return jnp.minimum(x, 0.0) - jnp.log(1.0 + jnp.exp(-jnp.abs(x)))


def _dot_nt(a, b):
    return lax.dot_general(a, b, (((1,), (1,)), ((), ())), preferred_element_type=F32)


def _dot_tn(a, b, precision=None):
    return lax.dot_general(a, b, (((0,), (0,)), ((), ())), precision=precision, preferred_element_type=F32)


def _head_rms(x, gain):
    return x * lax.rsqrt(jnp.mean(x * x, axis=1, keepdims=True) + HEAD_NORM_EPS) * gain


def _mixer_kernel(big_ref, small_ref, convw_ref, gbias_ref, w2_ref, glab_ref, mng_ref, gng_ref, o_ref,
                  ext_s, qk_s, c_s, n_s, m_s, s_s, *, tb):
    i = pl.program_id(0)
    nqk = 2 * MLSTM_HEADS * MLSTM_DK
    tail = CONV_WIDTH - 1

    @pl.when(i == 0)
    def _():
        ext_s[0:8, :] = jnp.zeros((8, nqk), F32)
        c_s[...] = jnp.zeros_like(c_s)
        n_s[...] = jnp.zeros_like(n_s)
        m_s[...] = jnp.zeros_like(m_s)
        s_s[...] = jnp.zeros_like(s_s)

    ext_s[8:8 + tb, :] = big_ref[:, E_MQ:E_MQ + nqk].astype(F32)
    conv = ext_s[8:8 + tb, :] * convw_ref[tail:tail + 1, :]
    for j in range(tail):
        conv = conv + ext_s[8 - tail + j:8 - tail + j + tb, :] * convw_ref[j:j + 1, :]
    qk_s[...] = conv * jax.nn.sigmoid(conv)
    ext_s[0:8, :] = ext_s[tb:tb + 8, :]

    rows = lax.broadcasted_iota(jnp.int32, (CHUNK, CHUNK), 0)
    cols = lax.broadcasted_iota(jnp.int32, (CHUNK, CHUNK), 1)
    causal = rows >= cols
    ltri = causal.astype(F32)
    eye = (rows == cols).astype(F32)
    lane = lax.broadcasted_iota(jnp.int32, (CHUNK, LANES), 1)
    is_f = (lane >= S_MF) & (lane < S_MF + MLSTM_HEADS)

    def chunk(c, carry):
        r0 = pl.multiple_of(c * CHUNK, CHUNK)
        rs = pl.ds(r0, CHUNK)
        gates = small_ref[rs, :]
        pre = gates + gbias_ref[...]
        xg = jnp.where(is_f, _log_sigmoid(pre), pre)
        cum = jnp.dot(ltri, xg, precision=HIGHEST, preferred_element_type=F32)
        y = jnp.where(is_f, cum, xg)
        yt = _dot_tn(y, eye, precision=HIGHEST)

        for h in range(MLSTM_HEADS):
            q = qk_s[rs, h * MLSTM_DK:(h + 1) * MLSTM_DK]
            k = qk_s[rs, (MLSTM_HEADS + h) * MLSTM_DK:(MLSTM_HEADS + h + 1) * MLSTM_DK] * MLSTM_DK ** -0.5
            v = big_ref[rs, E_MV + h * MLSTM_DV:E_MV + (h + 1) * MLSTM_DV]
            ig_c = y[:, S_MI + h:S_MI + h + 1]
            b_c = y[:, S_MF + h:S_MF + h + 1]
            ig_r = yt[S_MI + h:S_MI + h + 1, :]
            b_r = yt[S_MF + h:S_MF + h + 1, :]
            b_last = y[CHUNK - 1:CHUNK, S_MF + h:S_MF + h + 1]
            c_prev = c_s[h]
            n_prev = n_s[h:h + 1, :]
            m_prev = m_s[h:h + 1, 0:1]
            d_log = jnp.where(causal, b_c - b_r + ig_r, -jnp.inf)
            inter = b_c + m_prev
            m_q = jnp.maximum(inter, jnp.max(d_log, axis=1, keepdims=True))
            qb = q.astype(BF16)
            w = _dot_nt(qb, k.astype(BF16)) * jnp.exp(d_log - m_q)
            s_int = jnp.exp(inter - m_q)
            num = jnp.dot(w.astype(BF16), v, preferred_element_type=F32) + s_int * jnp.dot(
                qb, c_prev.astype(BF16), preferred_element_type=F32)
            den = jnp.sum(w, axis=1, keepdims=True) + s_int * jnp.sum(q * n_prev, axis=1, keepdims=True)
            out = num / jnp.maximum(jnp.abs(den), jnp.exp(-m_q))
            a_c = b_last - b_c + ig_c
            m_loc = jnp.max(a_c, axis=0, keepdims=True)
            kw = k * jnp.exp(a_c - m_loc)
            c_loc = _dot_tn(kw.astype(BF16), v)
            n_loc = jnp.sum(kw, axis=0, keepdims=True)
            m_new = jnp.maximum(b_last + m_prev, m_loc)
            s_old = jnp.exp(b_last + m_prev - m_new)
            s_loc = jnp.exp(m_loc - m_new)
            c_s[h] = s_old * c_prev + s_loc * c_loc
            n_s[h:h + 1, :] = s_old * n_prev + s_loc * n_loc
            m_s[h:h + 1, :] = jnp.broadcast_to(m_new, (1, LANES))
            hn = _head_rms(out, mng_ref[:, h * MLSTM_DV:(h + 1) * MLSTM_DV])
            og = big_ref[rs, E_MO + h * MLSTM_DV:E_MO + (h + 1) * MLSTM_DV].astype(F32)
            o_ref[rs, h * MLSTM_DV:(h + 1) * MLSTM_DV] = (hn * jax.nn.sigmoid(og)).astype(BF16)

        z = jnp.dot(gates, w2_ref[...], precision=HIGHEST, preferred_element_type=F32) + glab_ref[...]
        log_a = _log_sigmoid(z) / GLA_GATE_TAU
        la_all = jnp.dot(ltri, log_a, precision=HIGHEST, preferred_element_type=F32)
        for h in range(GLA_HEADS):
            la = la_all[:, h * GLA_DK:(h + 1) * GLA_DK]
            la_last = la[CHUNK - 1:CHUNK, :]
            q = big_ref[rs, E_GQ + h * GLA_DK:E_GQ + (h + 1) * GLA_DK].astype(F32) * GLA_DK ** -0.5
            k = big_ref[rs, E_GK + h * GLA_DK:E_GK + (h + 1) * GLA_DK].astype(F32)
            v = big_ref[rs, E_GV + h * GLA_DV:E_GV + (h + 1) * GLA_DV]
            q_dec = (q * jnp.exp(la)).astype(BF16)
            k_dec = (k * jnp.exp(-la)).astype(BF16)
            k_end = (k * jnp.exp(la_last - la)).astype(BF16)
            attn = jnp.where(causal, _dot_nt(q_dec, k_dec), 0.0)
            st_prev = s_s[h]
            out = jnp.dot(attn.astype(BF16), v, preferred_element_type=F32) + _dot_nt(q_dec, st_prev.astype(BF16))
            s_s[h] = jnp.exp(la_last) * st_prev + _dot_tn(v, k_end)
            hn = _head_rms(out, gng_ref[:, h * GLA_DV:(h + 1) * GLA_DV])
            gr = big_ref[rs, E_GR + h * GLA_DV:E_GR + (h + 1) * GLA_DV].astype(F32)
            col = MLSTM_HEADS * MLSTM_DV + h * GLA_DV
            o_ref[rs, col:col + GLA_DV] = (hn * (gr * jax.nn.sigmoid(gr))).astype(BF16)
        return carry

    lax.fori_loop(0, tb // CHUNK, chunk, 0)


def _mixer(big, small, convw, gbias, w2pad, glab, mng, gng, tb=512):
    t = big.shape[0]
    nqk = 2 * MLSTM_HEADS * MLSTM_DK
    fixed = lambda i: (0, 0)
    return pl.pallas_call(
        functools.partial(_mixer_kernel, tb=tb),
        grid=(t // tb,),
        in_specs=[pl.BlockSpec((tb, E_BIG), lambda i: (i, 0)), pl.BlockSpec((tb, LANES), lambda i: (i, 0)),
                  pl.BlockSpec((CONV_WIDTH, nqk), fixed), pl.BlockSpec((1, LANES), fixed),
                  pl.BlockSpec((LANES, GLA_HEADS * GLA_DK), fixed), pl.BlockSpec((1, GLA_HEADS * GLA_DK), fixed),
                  pl.BlockSpec((1, MLSTM_HEADS * MLSTM_DV), fixed), pl.BlockSpec((1, GLA_HEADS * GLA_DV), fixed)],
        out_specs=pl.BlockSpec((tb, MIX_WIDTH), lambda i: (i, 0)),
        out_shape=jax.ShapeDtypeStruct((t, MIX_WIDTH), BF16),
        scratch_shapes=[pltpu.VMEM((tb + 8, nqk), F32), pltpu.VMEM((tb, nqk), F32),
                        pltpu.VMEM((MLSTM_HEADS, MLSTM_DK, MLSTM_DV), F32), pltpu.VMEM((8, MLSTM_DK), F32),
                        pltpu.VMEM((8, LANES), F32), pltpu.VMEM((GLA_HEADS, GLA_DV, GLA_DK), F32)],
        compiler_params=_params(("arbitrary",)),
    )(big, small, convw, gbias, w2pad, glab, mng, gng)


def _rope_kernel(big_ref, small_ref, pos_ref, freq_ref, sign_ref, q_ref, k_ref, qi_ref, kit_ref, wi_ref):
    pos = pos_ref[...].astype(F32)
    lane = lax.broadcasted_iota(jnp.int32, (pos.shape[0], LANES), 1)

    def tables(row):
        ang = pos * freq_ref[row:row + 1, :]
        return jnp.cos(ang), jnp.sin(ang) * sign_ref[row:row + 1, :]

    def rotate(x, cos, sin, half, period):
        up = pltpu.roll(x, LANES - half, axis=1)
        dn = pltpu.roll(x, half, axis=1)
        return x * cos + jnp.where((lane % period) < half, up, dn) * sin

    cos_a, sin_a = tables(0)
    for h in range(ATTN_HEADS):
        x = big_ref[:, O_Q + h * ATTN_DH:O_Q + (h + 1) * ATTN_DH].astype(F32)
        r = rotate(x, cos_a, sin_a, ROPE_DIM // 2, ATTN_DH) * QK_SCALE_LOG2
        q_ref[:, h * ATTN_DH:(h + 1) * ATTN_DH] = r.astype(BF16)
    for h in range(ATTN_KV_HEADS):
        x = big_ref[:, O_K + h * ATTN_DH:O_K + (h + 1) * ATTN_DH].astype(F32)
        k_ref[:, h * ATTN_DH:(h + 1) * ATTN_DH] = rotate(x, cos_a, sin_a, ROPE_DIM // 2, ATTN_DH).astype(BF16)
    cos_i, sin_i = tables(1)
    for h in range(IDX_WIDTH // LANES):
        x = big_ref[:, O_QI + h * LANES:O_QI + (h + 1) * LANES].astype(F32)
        qi_ref[:, h * LANES:(h + 1) * LANES] = rotate(x, cos_i, sin_i, IDX_ROPE_DIM // 2, IDX_DH).astype(BF16)
    cos_k, sin_k = tables(2)
    sm = small_ref[...]
    ki = rotate(sm, cos_k, sin_k, IDX_ROPE_DIM // 2, IDX_DH)[:, :IDX_DH].astype(BF16)
    eye = (lax.broadcasted_iota(jnp.int32, (IDX_DH, IDX_DH), 0)
           == lax.broadcasted_iota(jnp.int32, (IDX_DH, IDX_DH), 1)).astype(BF16)
    kit_ref[...] = _dot_nt(eye, ki).astype(BF16)
    wi_ref[...] = pltpu.roll(sm, LANES - IDX_DH, axis=1) * (IDX_HEADS ** -0.5 * IDX_DH ** -0.5)


def _rope_tables():
    def pattern(rot_dim, period, width):
        half = rot_dim // 2
        inv = 1.0 / (ROPE_THETA ** (jnp.arange(half, dtype=F32) / half))
        l = np.arange(LANES)
        inside = ((l % period) < rot_dim) & (l < width)
        freq = jnp.where(inside, inv[(l % period) % half], 0.0)
        sign = np.where(inside, np.where((l % period) < half, -1.0, 1.0), 0.0)
        return freq, jnp.asarray(sign, F32)

    fa, sa = pattern(ROPE_DIM, ATTN_DH, LANES)
    fi, si = pattern(IDX_ROPE_DIM, IDX_DH, LANES)
    fk, sk = pattern(IDX_ROPE_DIM, IDX_DH, IDX_DH)
    z = jnp.zeros((5, LANES), F32)
    return jnp.concatenate([jnp.stack([fa, fi, fk]), z]), jnp.concatenate([jnp.stack([sa, si, sk]), z])


def _rope(big, small, pos, tm=512):
    t = big.shape[0]
    freq, sign = _rope_tables()
    row = lambda i: (i, 0)
    fixed = lambda i: (0, 0)
    return pl.pallas_call(
        _rope_kernel,
        grid=(t // tm,),
        in_specs=[pl.BlockSpec((tm, O_BIG), row), pl.BlockSpec((tm, LANES), row), pl.BlockSpec((tm, 1), row),
                  pl.BlockSpec((8, LANES), fixed), pl.BlockSpec((8, LANES), fixed)],
        out_specs=[pl.BlockSpec((tm, ATTN_WIDTH), row), pl.BlockSpec((tm, KV_WIDTH), row),
                   pl.BlockSpec((tm, IDX_WIDTH), row), pl.BlockSpec((IDX_DH, tm), lambda i: (0, i)),
                   pl.BlockSpec((tm, LANES), row)],
        out_shape=[jax.ShapeDtypeStruct((t, ATTN_WIDTH), BF16), jax.ShapeDtypeStruct((t, KV_WIDTH), BF16),
                   jax.ShapeDtypeStruct((t, IDX_WIDTH), BF16), jax.ShapeDtypeStruct((IDX_DH, t), BF16),
                   jax.ShapeDtypeStruct((t, LANES), F32)],
        compiler_params=_params(("parallel",)),
        name="rope",
    )(big, small, pos, freq, sign)


def _dsa_kernel(q_ref, qi_ref, wi_ref, kit_ref, k_ref, v_ref, o_ref,
                keys_s, bias_s, p_s, alpha_s, m_s, acc_s, *, tq, tk, top_k):
    qb = pl.program_id(0)
    n_tiles = (qb * tq + tq - 1) // tk + 1

    w = wi_ref[...]
    qpos = qb * tq + lax.broadcasted_iota(jnp.int32, (tq, tk), 0)
    kiota = lax.broadcasted_iota(jnp.int32, (tq, tk), 1)

    def score_tile(c, carry):
        c0 = pl.multiple_of(c * tk, tk)
        kt = kit_ref[:, pl.ds(c0, tk)]
        sc = jnp.zeros((tq, tk), F32)
        for h in range(IDX_HEADS):
            lg = jnp.dot(qi_ref[:, h * IDX_DH:(h + 1) * IDX_DH], kt, preferred_element_type=F32)
            sc = sc + jnp.maximum(lg, 0.0) * w[:, h:h + 1]
        bits = pltpu.bitcast(sc, jnp.int32)
        key = bits ^ ((bits >> 31) & 0x7FFFFFFF)
        keys_s[:, pl.ds(c0, tk)] = jnp.where(kiota + c0 <= qpos, key, INT_MIN)
        return carry

    lax.fori_loop(0, n_tiles, score_tile, 0)

    def count_ge(cand):
        def body(c, acc):
            c0 = pl.multiple_of(c * tk, tk)
            ge = (keys_s[:, pl.ds(c0, tk)] >= cand).astype(jnp.int32)
            for u in range(tk // LANES):
                acc = acc + ge[:, u * LANES:(u + 1) * LANES]
            return acc

        acc = lax.fori_loop(0, n_tiles, body, jnp.zeros((tq, LANES), jnp.int32))
        return jnp.sum(acc, axis=1, keepdims=True)

    zero = jnp.zeros((tq, 1), jnp.int32)
    cnt0 = count_ge(zero)
    nonneg = cnt0 >= top_k
    base0 = jnp.where(nonneg, zero, INT_MIN)
    kept0 = jnp.where(nonneg, cnt0, n_tiles * tk)

    def unsettled(state):
        b, _, kept = state
        return jnp.logical_and(b < 31, jnp.max(jnp.abs(kept - top_k)) > 0)

    def bit_step(state):
        b, base, kept = state
        cand = base | jnp.left_shift(jnp.int32(1), 30 - b)
        cnt = count_ge(cand)
        ok = cnt >= top_k
        return b + 1, jnp.where(ok, cand, base), jnp.where(ok, cnt, kept)

    _, thr, _ = lax.while_loop(unsettled, bit_step, (jnp.int32(0), base0, kept0))
    thr = jnp.maximum(thr, INT_MIN + 1)

    m_s[...] = jnp.full_like(m_s, NEG_BIG)
    acc_s[...] = jnp.zeros_like(acc_s)
    ones = jnp.ones((tk, ATTN_DH), BF16)

    def flash_tile(c, carry):
        c0 = pl.multiple_of(c * tk, tk)
        bias_s[...] = jnp.where(keys_s[:, pl.ds(c0, tk)] >= thr, 0.0, NEG_BIG)
        for g in range(ATTN_KV_HEADS):
            qg = jnp.concatenate(
                [q_ref[:, (g * Q_PER_KV + r) * ATTN_DH:(g * Q_PER_KV + r + 1) * ATTN_DH] for r in range(Q_PER_KV)],
                axis=0)
            s = _dot_nt(qg, k_ref[pl.ds(c0, tk), g * ATTN_DH:(g + 1) * ATTN_DH])
            for r in range(Q_PER_KV):
                rows = slice(r * tq, (r + 1) * tq)
                sr = s[rows, :] + bias_s[...]
                m_prev = m_s[g, rows, :]
                m_new = jnp.maximum(m_prev, jnp.max(sr, axis=1, keepdims=True))
                p_s[rows, :] = jnp.exp2(sr - jnp.tile(m_new, (1, tk // LANES))).astype(BF16)
                alpha_s[rows, :] = jnp.exp2(m_prev - m_new)
                m_s[g, rows, :] = m_new
            v_ext = jnp.concatenate([v_ref[pl.ds(c0, tk), g * ATTN_DH:(g + 1) * ATTN_DH], ones], axis=1)
            pv = jnp.dot(p_s[...], v_ext, preferred_element_type=F32)
            acc_s[g] = jnp.tile(alpha_s[...], (1, 2)) * acc_s[g] + pv
        return carry

    lax.fori_loop(0, n_tiles, flash_tile, 0)

    for g in range(ATTN_KV_HEADS):
        a = acc_s[g]
        res = a[:, :ATTN_DH] / a[:, ATTN_DH:]
        for r in range(Q_PER_KV):
            hcol = (g * Q_PER_KV + r) * ATTN_DH
            o_ref[:, hcol:hcol + ATTN_DH] = res[r * tq:(r + 1) * tq, :].astype(BF16)


def _dsa(q, qi, wi, kit, k, big, tq=128, tk=512):
    t = q.shape[0]
    top_k = min(INDEX_TOPK, t // 4)
    v_blk = O_V // KV_WIDTH
    qrow = lambda i: (i, 0)
    once = pl.Buffered(1)
    rq = Q_PER_KV * tq
    return pl.pallas_call(
        functools.partial(_dsa_kernel, tq=tq, tk=tk, top_k=top_k),
        grid=(t // tq,),
        in_specs=[pl.BlockSpec((tq, ATTN_WIDTH), qrow), pl.BlockSpec((tq, IDX_WIDTH), qrow),
                  pl.BlockSpec((tq, LANES), qrow),
                  pl.BlockSpec((IDX_DH, t), lambda i: (0, 0), pipeline_mode=once),
                  pl.BlockSpec((t, KV_WIDTH), lambda i: (0, 0), pipeline_mode=once),
                  pl.BlockSpec((t, KV_WIDTH), lambda i: (0, v_blk), pipeline_mode=once)],
        out_specs=pl.BlockSpec((tq, ATTN_WIDTH), qrow),
        out_shape=jax.ShapeDtypeStruct((t, ATTN_WIDTH), BF16),
        scratch_shapes=[pltpu.VMEM((tq, t), jnp.int32), pltpu.VMEM((tq, tk), F32), pltpu.VMEM((rq, tk), BF16),
                        pltpu.VMEM((rq, LANES), F32), pltpu.VMEM((ATTN_KV_HEADS, rq, LANES), F32),
                        pltpu.VMEM((ATTN_KV_HEADS, rq, 2 * ATTN_DH), F32)],
        compiler_params=pltpu.CompilerParams(dimension_semantics=("parallel",), vmem_limit_bytes=DSA_VMEM_LIMIT),
        name="dsa_attention",
    )(q, qi, wi, kit, k, big)


def _pad_cols(w, width):
    return jnp.pad(w, ((0, 0), (0, width - w.shape[1])))


def _even_weights(w_in, igate_b, fgate_b, gla_w2, gla_b):
    cuts = np.cumsum([512, 512, 1024, 4, 4, 1024, 512, 512, 1024, 16, 1024])[:-1]
    mq, mk, mv, mi, mf, mo, gq, gk, gv, glr, gr = jnp.split(w_in, [int(c) for c in cuts], axis=1)
    w_big = jnp.concatenate([mq, mk, mv, mo, gq, gk, gv, gr], axis=1).astype(BF16)
    w_small = _pad_cols(jnp.concatenate([mi, mf, glr], axis=1), LANES).astype(BF16)
    gbias = _pad_cols(jnp.concatenate([igate_b, fgate_b])[None, :], LANES)
    w2pad = jnp.zeros((LANES, GLA_HEADS * GLA_DK), F32).at[S_GLR:S_GLR + GLA_GATE_RANK].set(gla_w2)
    return w_big, w_small, gbias, w2pad, gla_b[None, :]


def _odd_weights(w_in):
    cuts = np.cumsum([ATTN_WIDTH, KV_WIDTH, KV_WIDTH, IDX_WIDTH, IDX_DH, IDX_HEADS])[:-1]
    q, k, v, qi, ki, wi = jnp.split(w_in, [int(c) for c in cuts], axis=1)
    w_big = jnp.concatenate([q, k, v, qi], axis=1).astype(BF16)
    w_small = _pad_cols(jnp.concatenate([ki, wi], axis=1), LANES).astype(BF16)
    return w_big, w_small


def _even_mixer(hb, w_in, conv_w, igate_b, fgate_b, gla_w2, gla_b, mng, gng):
    w_big, w_small, gbias, w2pad, glab = _even_weights(w_in, igate_b, fgate_b, gla_w2, gla_b)
    big = _matmul(hb, w_big, BF16, 1024, 1024)
    small = _matmul(hb, w_small, F32, 1024, LANES)
    return _mixer(big, small, conv_w, gbias, w2pad, glab, mng[None, :], gng[None, :])


def _odd_mixer(hb, pos, w_in):
    w_big, w_small = _odd_weights(w_in)
    big = _matmul(hb, w_big, BF16, 1024, 1024)
    small = _matmul(hb, w_small, F32, 1024, LANES)
    q, k, qi, kit, wi = _rope(big, small, pos)
    return _dsa(q, qi, wi, kit, k, big)


def kernel(x, positions, ab_w_in, ab_conv_w, ab_igate_b, ab_fgate_b, ab_gla_gate_w2, ab_gla_gate_b, ab_mlstm_norm_g, ab_gla_norm_g, ab_w_out, c_w_in, c_w_out, router_w, router_bias, moe_w_gate, moe_w_up, moe_w_down, ln_mix_g, ln_mix_b, ln_ffn_g, ln_ffn_b):
    bsz, t, d = x.shape
    outs = []
    rwt = router_w.T
    rbias = router_bias[:, None]
    for bi in range(bsz):
        h = x[bi]
        hb = h.astype(BF16)
        pos = positions[bi][:, None]
        for layer in range(DEPTH):
            j = layer // 2
            if layer % 2 == 0:
                mix = _even_mixer(hb, ab_w_in[j], ab_conv_w[j], ab_igate_b[j], ab_fgate_b[j], ab_gla_gate_w2[j],
                                  ab_gla_gate_b[j], ab_mlstm_norm_g[j], ab_gla_norm_g[j])
                w_out = ab_w_out[j]
            else:
                mix = _odd_mixer(hb, pos, c_w_in[j])
                w_out = c_w_out[j]
            h, hb = _outproj_ln(mix, w_out.astype(BF16), h, ln_mix_g[layer], ln_mix_b[layer])
            _, _, comb_t = _router(h, rwt, rbias)
            h, hb = _moe(hb, comb_t.T, moe_w_gate[layer].astype(BF16), moe_w_up[layer].astype(BF16),
                         moe_w_down[layer].astype(BF16), h, ln_ffn_g[layer], ln_ffn_b[layer])
        outs.append(h)
    return jnp.stack(outs)
```

```python
import functools

import jax
import jax.numpy as jnp
import numpy as np
from jax import lax
from jax.experimental import pallas as pl
from jax.experimental.pallas import tpu as pltpu

F32 = jnp.float32
BF16 = jnp.bfloat16
HIGHEST = lax.Precision.HIGHEST

D_MODEL = 2048
DEPTH = 4
MLSTM_HEADS = 4
MLSTM_DK = 128
MLSTM_DV = 256
CONV_WIDTH = 4
GLA_HEADS = 4
GLA_DK = 128
GLA_DV = 256
GLA_GATE_RANK = 16
GLA_GATE_TAU = 16.0
CHUNK = 64
ATTN_HEADS = 16
ATTN_KV_HEADS = 4
ATTN_DH = 128
ROPE_DIM = ATTN_DH // 4
ROPE_THETA = 500000.0
IDX_HEADS = 16
IDX_DH = 64
IDX_ROPE_DIM = IDX_DH // 4
INDEX_TOPK = 256
N_EXPERTS = 16
N_GROUPS = 4
EXPERTS_PER_GROUP = N_EXPERTS // N_GROUPS
EXPERT_FF = 512
DEEPNORM_ALPHA = (2 * DEPTH) ** 0.25
LN_EPS = 1e-5
HEAD_NORM_EPS = 1e-6

LANES = 128
MIX_WIDTH = MLSTM_HEADS * MLSTM_DV + GLA_HEADS * GLA_DV
ATTN_WIDTH = ATTN_HEADS * ATTN_DH
KV_WIDTH = ATTN_KV_HEADS * ATTN_DH
IDX_WIDTH = IDX_HEADS * IDX_DH
Q_PER_KV = ATTN_HEADS // ATTN_KV_HEADS

E_MQ = 0
E_MK = E_MQ + MLSTM_HEADS * MLSTM_DK
E_MV = E_MK + MLSTM_HEADS * MLSTM_DK
E_MO = E_MV + MLSTM_HEADS * MLSTM_DV
E_GQ = E_MO + MLSTM_HEADS * MLSTM_DV
E_GK = E_GQ + GLA_HEADS * GLA_DK
E_GV = E_GK + GLA_HEADS * GLA_DK
E_GR = E_GV + GLA_HEADS * GLA_DV
E_BIG = E_GR + GLA_HEADS * GLA_DV
S_MI = 0
S_MF = S_MI + MLSTM_HEADS
S_GLR = S_MF + MLSTM_HEADS
O_Q = 0
O_K = O_Q + ATTN_WIDTH
O_V = O_K + KV_WIDTH
O_QI = O_V + KV_WIDTH
O_BIG = O_QI + IDX_WIDTH

VMEM_LIMIT = 56 * 1024 * 1024
DSA_VMEM_LIMIT = 60 * 1024 * 1024
QK_SCALE_LOG2 = ATTN_DH ** -0.5 * 1.4426950408889634
INT_MIN = -(2**31)
NEG_BIG = -1e30


def _params(sem):
    return pltpu.CompilerParams(dimension_semantics=sem, vmem_limit_bytes=VMEM_LIMIT)


def _mm_kernel(x_ref, w_ref, o_ref):
    o_ref[...] = jnp.dot(x_ref[...], w_ref[...], preferred_element_type=F32).astype(o_ref.dtype)


def _matmul(x, w, out_dtype, tm, tn):
    t, k = x.shape
    n = w.shape[1]
    return pl.pallas_call(
        _mm_kernel,
        grid=(t // tm, n // tn),
        in_specs=[pl.BlockSpec((tm, k), lambda i, j: (i, 0)), pl.BlockSpec((k, tn), lambda i, j: (0, j))],
        out_specs=pl.BlockSpec((tm, tn), lambda i, j: (i, j)),
        out_shape=jax.ShapeDtypeStruct((t, n), out_dtype),
        compiler_params=_params(("parallel", "parallel")),
    )(x, w)


def _layer_norm(z, g, b):
    mu = jnp.mean(z, axis=-1, keepdims=True)
    zc = z - mu
    var = jnp.mean(zc * zc, axis=-1, keepdims=True)
    return zc * lax.rsqrt(var + LN_EPS) * g + b


def _outproj_ln_kernel(mix_ref, w_ref, h_ref, g_ref, b_ref, of_ref, ob_ref):
    y = jnp.dot(mix_ref[...], w_ref[...], preferred_element_type=F32)
    out = _layer_norm(DEEPNORM_ALPHA * h_ref[...] + y, g_ref[...], b_ref[...])
    of_ref[...] = out
    ob_ref[...] = out.astype(BF16)


def _outproj_ln(mix, w, h, g, b, tm=256):
    t, k = mix.shape
    d = w.shape[1]
    row = lambda i: (i, 0)
    fixed = lambda i: (0, 0)
    return pl.pallas_call(
        _outproj_ln_kernel,
        grid=(t // tm,),
        in_specs=[pl.BlockSpec((tm, k), row), pl.BlockSpec((k, d), fixed), pl.BlockSpec((tm, d), row),
                  pl.BlockSpec((1, d), fixed), pl.BlockSpec((1, d), fixed)],
        out_specs=[pl.BlockSpec((tm, d), row), pl.BlockSpec((tm, d), row)],
        out_shape=[jax.ShapeDtypeStruct((t, d), F32), jax.ShapeDtypeStruct((t, d), BF16)],
        compiler_params=_params(("parallel",)),
    )(mix, w, h, g.reshape(1, d), b.reshape(1, d))


def _router_kernel(h_ref, rwt_ref, bias_ref, grp_ref, comb_ref):
    logits = lax.dot_general(rwt_ref[...], h_ref[...], (((1,), (1,)), ((), ())),
                             precision=HIGHEST, preferred_element_type=F32)
    aff = jax.nn.sigmoid(logits)
    sel = aff + bias_ref[...]
    srow = [sel[e:e + 1, :] for e in range(N_EXPERTS)]
    arow = [aff[e:e + 1, :] for e in range(N_EXPERTS)]

    def top2_sum(v):
        best = v[0] + v[1]
        for a in range(len(v)):
            for b in range(a + 1, len(v)):
                if (a, b) != (0, 1):
                    best = jnp.maximum(best, v[a] + v[b])
        return best

    gscore = [top2_sum(srow[g * EXPERTS_PER_GROUP:(g + 1) * EXPERTS_PER_GROUP]) for g in range(N_GROUPS)]
    best = gscore[0]
    gi = jnp.zeros_like(best, dtype=jnp.int32)
    for g in range(1, N_GROUPS):
        upd = gscore[g] > best
        gi = jnp.where(upd, g, gi)
        best = jnp.where(upd, gscore[g], best)

    def pick(rows, k):
        out = rows[k]
        for g in range(1, N_GROUPS):
            out = jnp.where(gi == g, rows[g * EXPERTS_PER_GROUP + k], out)
        return out

    v = [pick(srow, k) for k in range(EXPERTS_PER_GROUP)]
    a = [pick(arow, k) for k in range(EXPERTS_PER_GROUP)]

    def first_argmax(vals):
        m = vals[0]
        for x in vals[1:]:
            m = jnp.maximum(m, x)
        loc = jnp.full_like(gi, len(vals) - 1)
        for k in range(len(vals) - 2, -1, -1):
            loc = jnp.where(vals[k] == m, k, loc)
        return loc

    loc1 = first_argmax(v)
    v2 = [jnp.where(loc1 == k, -jnp.inf, v[k]) for k in range(EXPERTS_PER_GROUP)]
    loc2 = first_argmax(v2)

    def take(vals, loc):
        out = vals[0]
        for k in range(1, len(vals)):
            out = jnp.where(loc == k, vals[k], out)
        return out

    g1 = take(a, loc1)
    g2 = take(a, loc2)
    tot = g1 + g2
    g1 = g1 / tot
    g2 = g2 / tot
    grp_ref[...] = gi
    krow = lax.broadcasted_iota(jnp.int32, comb_ref.shape, 0)
    comb_ref[...] = jnp.where(krow == loc1, g1, 0.0) + jnp.where(krow == loc2, g2, 0.0)


def _router(h, rwt, bias, tm=512):
    t, d = h.shape
    return pl.pallas_call(
        _router_kernel,
        grid=(t // tm,),
        in_specs=[pl.BlockSpec((tm, d), lambda i: (i, 0)), pl.BlockSpec((N_EXPERTS, d), lambda i: (0, 0)),
                  pl.BlockSpec((N_EXPERTS, 1), lambda i: (0, 0))],
        out_specs=[pl.BlockSpec((1, tm), lambda i: (0, i)), pl.BlockSpec((EXPERTS_PER_GROUP, tm), lambda i: (0, i))],
        out_shape=[jax.ShapeDtypeStruct((1, t), jnp.int32), jax.ShapeDtypeStruct((EXPERTS_PER_GROUP, t), F32)],
        compiler_params=_params(("parallel",)),
        name="router",
    )(h, rwt, bias)


def _cast_kernel(a_ref, b_ref, c_ref, ao_ref, bo_ref, co_ref):
    ao_ref[...] = a_ref[...].astype(BF16)
    bo_ref[...] = b_ref[...].astype(BF16)
    co_ref[...] = c_ref[...].astype(BF16)


def _cast_expert_weights(wg, wu, wd):
    n = wg.shape[0]
    spec = lambda a: pl.BlockSpec((1,) + a.shape[1:], lambda i: (i, 0, 0))
    return pl.pallas_call(
        _cast_kernel,
        grid=(n,),
        in_specs=[spec(wg), spec(wu), spec(wd)],
        out_specs=[spec(wg), spec(wu), spec(wd)],
        out_shape=[jax.ShapeDtypeStruct(a.shape, BF16) for a in (wg, wu, wd)],
        compiler_params=_params(("parallel",)),
        name="cast_expert_weights",
    )(wg, wu, wd)


def _moe_kernel(tg_ref, src_ref, nv_ref, nu_ref, h_hbm, comb_ref, wg_ref, wu_ref, wd_ref, y_hbm,
                xbuf, ybuf, gsem, ssem, *, tm):
    i = pl.program_id(0)
    e = pl.program_id(1)
    n_used = nu_ref[0]
    slot = lax.rem(i, 2)
    part = tm // EXPERTS_PER_GROUP

    def row_copy(r, tile):
        return pltpu.make_async_copy(ybuf.at[pl.ds(r, 1), :], y_hbm.at[pl.ds(src_ref[tile * tm + r], 1), :],
                                     ssem.at[0])

    def gather_rows(tile, dst_slot, lo, hi):
        def body(r, carry):
            tok = src_ref[tile * tm + r]
            pltpu.make_async_copy(h_hbm.at[pl.ds(tok, 1), :], xbuf.at[dst_slot, pl.ds(r, 1), :],
                                  gsem.at[dst_slot]).start()
            return carry

        lax.fori_loop(lo, hi, body, 0)

    def wait_gather(s):
        pltpu.make_async_copy(xbuf.at[s], xbuf.at[s], gsem.at[s]).wait()

    def wait_scatter(tile):
        def body(r, carry):
            row_copy(r, tile).wait()
            return carry

        lax.fori_loop(0, nv_ref[tile], body, 0)

    @pl.when(i < n_used)
    def _():
        @pl.when(jnp.logical_and(i == 0, e == 0))
        def _():
            gather_rows(0, 0, 0, tm)

        @pl.when(e == 0)
        def _():
            wait_gather(slot)

        @pl.when(i + 1 < n_used)
        def _():
            gather_rows(i + 1, 1 - slot, e * part, (e + 1) * part)

        comb = comb_ref[...]
        lane = lax.broadcasted_iota(jnp.int32, comb.shape, 1)
        c = jnp.sum(jnp.where(lane == e, comb, 0.0), axis=1, keepdims=True)
        x = xbuf[slot].astype(BF16)
        hg = jnp.dot(x, wg_ref[0], preferred_element_type=F32)
        hu = jnp.dot(x, wu_ref[0], preferred_element_type=F32)
        hh = (hg * jax.nn.sigmoid(hg) * hu) * c
        contrib = jnp.dot(hh.astype(BF16), wd_ref[0], preferred_element_type=F32)

        @pl.when(e == 0)
        def _():
            @pl.when(i > 0)
            def _():
                wait_scatter(i - 1)

            ybuf[...] = contrib

        @pl.when(e > 0)
        def _():
            ybuf[...] += contrib

        @pl.when(e == EXPERTS_PER_GROUP - 1)
        def _():
            def body(r, carry):
                row_copy(r, i).start()
                return carry

            lax.fori_loop(0, nv_ref[i], body, 0)

            @pl.when(i == n_used - 1)
            def _():
                wait_scatter(i)


def _moe(h, grp, comb, wg, wu, wd, first_expert, tm=512):
    t, d = h.shape
    f = wg.shape[-1]
    nt = t // tm + N_GROUPS
    p = nt * tm
    onehot = (grp[:, None] == jnp.arange(N_GROUPS)[None, :]).astype(jnp.int32)
    csum = jnp.cumsum(onehot, axis=0)
    counts = csum[-1]
    rank = jnp.sum(csum * onehot, axis=1) - 1
    padded = (counts + tm - 1) // tm * tm
    ends = jnp.cumsum(padded)
    starts = ends - padded
    pos = jnp.sum(starts[None, :] * onehot, axis=1) + rank
    tok = jnp.full((p,), -1, jnp.int32).at[pos].set(jnp.arange(t, dtype=jnp.int32))
    valid = tok >= 0
    src = jnp.where(valid, tok, 0)
    comb_s = jnp.where(valid[:, None], comb[src], 0.0)
    n_valid = jnp.sum(valid.reshape(nt, tm), axis=1).astype(jnp.int32)
    tile_group = jnp.minimum(jnp.searchsorted(ends, jnp.arange(nt, dtype=jnp.int32) * tm, side="right"),
                             N_GROUPS - 1).astype(jnp.int32)
    n_used = (ends[-1] // tm).astype(jnp.int32).reshape(1)

    wmap = lambda i, e, tg, s, nv, nu: (first_expert + tg[i] * EXPERTS_PER_GROUP + e, 0, 0)
    grid_spec = pltpu.PrefetchScalarGridSpec(
        num_scalar_prefetch=4,
        grid=(nt, EXPERTS_PER_GROUP),
        in_specs=[pl.BlockSpec(memory_space=pl.ANY),
                  pl.BlockSpec((tm, EXPERTS_PER_GROUP), lambda i, e, tg, s, nv, nu: (i, 0)),
                  pl.BlockSpec((1, d, f), wmap), pl.BlockSpec((1, d, f), wmap), pl.BlockSpec((1, f, d), wmap)],
        out_specs=pl.BlockSpec(memory_space=pl.ANY),
        scratch_shapes=[pltpu.VMEM((2, tm, d), F32), pltpu.VMEM((tm, d), F32),
                        pltpu.SemaphoreType.DMA((2,)), pltpu.SemaphoreType.DMA((1,))],
    )
    return pl.pallas_call(
        functools.partial(_moe_kernel, tm=tm),
        grid_spec=grid_spec,
        out_shape=jax.ShapeDtypeStruct((t, d), F32),
        compiler_params=_params(("arbitrary", "arbitrary")),
        name="moe_grouped",
    )(tile_group, src, n_valid, n_used, h, comb_s, wg, wu, wd)


def _residual_ln_kernel(h_ref, y_ref, g_ref, b_ref, of_ref, ob_ref):
    out = _layer_norm(DEEPNORM_ALPHA * h_ref[...] + y_ref[...], g_ref[...], b_ref[...])
    of_ref[...] = out
    ob_ref[...] = out.astype(BF16)


def _residual_ln(h, y, g, b, tm=512):
    t, d = h.shape
    row = lambda i: (i, 0)
    fixed = lambda i: (0, 0)
    return pl.pallas_call(
        _residual_ln_kernel,
        grid=(t // tm,),
        in_specs=[pl.BlockSpec((tm, d), row), pl.BlockSpec((tm, d), row), pl.BlockSpec((1, d), fixed),
                  pl.BlockSpec((1, d), fixed)],
        out_specs=[pl.BlockSpec((tm, d), row), pl.BlockSpec((tm, d), row)],
        out_shape=[jax.ShapeDtypeStruct((t, d), F32), jax.ShapeDtypeStruct((t, d), BF16)],
        compiler_params=_params(("parallel",)),
        name="residual_ln",
    )(h, y, g.reshape(1, d), b.reshape(1, d))


def _log_sigmoid(x):
    return jnp.minimum(x, 0.0) - jnp.log(1.0 + jnp.exp(-jnp.abs(x)))


def _dot_nt(a, b):
    return lax.dot_general(a, b, (((1,), (1,)), ((), ())), preferred_element_type=F32)


def _dot_tn(a, b, precision=None):
    return lax.dot_general(a, b, (((0,), (0,)), ((), ())), precision=precision, preferred_element_type=F32)


def _head_rms(x, gain):
    return x * lax.rsqrt(jnp.mean(x * x, axis=1, keepdims=True) + HEAD_NORM_EPS) * gain


def _mixer_kernel(big_ref, small_ref, convw_ref, gbias_ref, w2_ref, glab_ref, mng_ref, gng_ref, o_ref,
                  ext_s, qk_s, c_s, n_s, m_s, s_s, *, tb):
    i = pl.program_id(0)
    nqk = 2 * MLSTM_HEADS * MLSTM_DK
    tail = CONV_WIDTH - 1

    @pl.when(i == 0)
    def _():
        ext_s[0:8, :] = jnp.zeros((8, nqk), F32)
        c_s[...] = jnp.zeros_like(c_s)
        n_s[...] = jnp.zeros_like(n_s)
        m_s[...] = jnp.zeros_like(m_s)
        s_s[...] = jnp.zeros_like(s_s)

    ext_s[8:8 + tb, :] = big_ref[:, E_MQ:E_MQ + nqk].astype(F32)
    conv = ext_s[8:8 + tb, :] * convw_ref[tail:tail + 1, :]
    for j in range(tail):
        conv = conv + ext_s[8 - tail + j:8 - tail + j + tb, :] * convw_ref[j:j + 1, :]
    qk_s[...] = conv * jax.nn.sigmoid(conv)
    ext_s[0:8, :] = ext_s[tb:tb + 8, :]

    rows = lax.broadcasted_iota(jnp.int32, (CHUNK, CHUNK), 0)
    cols = lax.broadcasted_iota(jnp.int32, (CHUNK, CHUNK), 1)
    causal = rows >= cols
    ltri = causal.astype(F32)
    eye = (rows == cols).astype(F32)
    lane = lax.broadcasted_iota(jnp.int32, (CHUNK, LANES), 1)
    is_f = (lane >= S_MF) & (lane < S_MF + MLSTM_HEADS)

    def chunk(c, carry):
        r0 = pl.multiple_of(c * CHUNK, CHUNK)
        rs = pl.ds(r0, CHUNK)
        gates = small_ref[rs, :]
        pre = gates + gbias_ref[...]
        xg = jnp.where(is_f, _log_sigmoid(pre), pre)
        cum = jnp.dot(ltri, xg, precision=HIGHEST, preferred_element_type=F32)
        y = jnp.where(is_f, cum, xg)
        yt = _dot_tn(y, eye, precision=HIGHEST)

        for h in range(MLSTM_HEADS):
            q = qk_s[rs, h * MLSTM_DK:(h + 1) * MLSTM_DK]
            k = qk_s[rs, (MLSTM_HEADS + h) * MLSTM_DK:(MLSTM_HEADS + h + 1) * MLSTM_DK] * MLSTM_DK ** -0.5
            v = big_ref[rs, E_MV + h * MLSTM_DV:E_MV + (h + 1) * MLSTM_DV]
            ig_c = y[:, S_MI + h:S_MI + h + 1]
            b_c = y[:, S_MF + h:S_MF + h + 1]
            ig_r = yt[S_MI + h:S_MI + h + 1, :]
            b_r = yt[S_MF + h:S_MF + h + 1, :]
            b_last = y[CHUNK - 1:CHUNK, S_MF + h:S_MF + h + 1]
            c_prev = c_s[h]
            n_prev = n_s[h:h + 1, :]
            m_prev = m_s[h:h + 1, 0:1]
            d_log = jnp.where(causal, b_c - b_r + ig_r, -jnp.inf)
            inter = b_c + m_prev
            m_q = jnp.maximum(inter, jnp.max(d_log, axis=1, keepdims=True))
            qb = q.astype(BF16)
            w = _dot_nt(qb, k.astype(BF16)) * jnp.exp(d_log - m_q)
            s_int = jnp.exp(inter - m_q)
            num = jnp.dot(w.astype(BF16), v, preferred_element_type=F32) + s_int * jnp.dot(
                qb, c_prev.astype(BF16), preferred_element_type=F32)
            den = jnp.sum(w, axis=1, keepdims=True) + s_int * jnp.sum(q * n_prev, axis=1, keepdims=True)
            out = num / jnp.maximum(jnp.abs(den), jnp.exp(-m_q))
            a_c = b_last - b_c + ig_c
            m_loc = jnp.max(a_c, axis=0, keepdims=True)
            kw = k * jnp.exp(a_c - m_loc)
            c_loc = _dot_tn(kw.astype(BF16), v)
            n_loc = jnp.sum(kw, axis=0, keepdims=True)
            m_new = jnp.maximum(b_last + m_prev, m_loc)
            s_old = jnp.exp(b_last + m_prev - m_new)
            s_loc = jnp.exp(m_loc - m_new)
            c_s[h] = s_old * c_prev + s_loc * c_loc
            n_s[h:h + 1, :] = s_old * n_prev + s_loc * n_loc
            m_s[h:h + 1, :] = jnp.broadcast_to(m_new, (1, LANES))
            hn = _head_rms(out, mng_ref[:, h * MLSTM_DV:(h + 1) * MLSTM_DV])
            og = big_ref[rs, E_MO + h * MLSTM_DV:E_MO + (h + 1) * MLSTM_DV].astype(F32)
            o_ref[rs, h * MLSTM_DV:(h + 1) * MLSTM_DV] = (hn * jax.nn.sigmoid(og)).astype(BF16)

        z = jnp.dot(gates, w2_ref[...], precision=HIGHEST, preferred_element_type=F32) + glab_ref[...]
        log_a = _log_sigmoid(z) / GLA_GATE_TAU
        la_all = jnp.dot(ltri, log_a, precision=HIGHEST, preferred_element_type=F32)
        for h in range(GLA_HEADS):
            la = la_all[:, h * GLA_DK:(h + 1) * GLA_DK]
            la_last = la[CHUNK - 1:CHUNK, :]
            q = big_ref[rs, E_GQ + h * GLA_DK:E_GQ + (h + 1) * GLA_DK].astype(F32) * GLA_DK ** -0.5
            k = big_ref[rs, E_GK + h * GLA_DK:E_GK + (h + 1) * GLA_DK].astype(F32)
            v = big_ref[rs, E_GV + h * GLA_DV:E_GV + (h + 1) * GLA_DV]
            q_dec = (q * jnp.exp(la)).astype(BF16)
            k_dec = (k * jnp.exp(-la)).astype(BF16)
            k_end = (k * jnp.exp(la_last - la)).astype(BF16)
            attn = jnp.where(causal, _dot_nt(q_dec, k_dec), 0.0)
            st_prev = s_s[h]
            out = jnp.dot(attn.astype(BF16), v, preferred_element_type=F32) + _dot_nt(q_dec, st_prev.astype(BF16))
            s_s[h] = jnp.exp(la_last) * st_prev + _dot_tn(v, k_end)
            hn = _head_rms(out, gng_ref[:, h * GLA_DV:(h + 1) * GLA_DV])
            gr = big_ref[rs, E_GR + h * GLA_DV:E_GR + (h + 1) * GLA_DV].astype(F32)
            col = MLSTM_HEADS * MLSTM_DV + h * GLA_DV
            o_ref[rs, col:col + GLA_DV] = (hn * (gr * jax.nn.sigmoid(gr))).astype(BF16)
        return carry

    lax.fori_loop(0, tb // CHUNK, chunk, 0)


def _mixer(big, small, convw, gbias, w2pad, glab, mng, gng, tb=512):
    t = big.shape[0]
    nqk = 2 * MLSTM_HEADS * MLSTM_DK
    fixed = lambda i: (0, 0)
    return pl.pallas_call(
        functools.partial(_mixer_kernel, tb=tb),
        grid=(t // tb,),
        in_specs=[pl.BlockSpec((tb, E_BIG), lambda i: (i, 0)), pl.BlockSpec((tb, LANES), lambda i: (i, 0)),
                  pl.BlockSpec((CONV_WIDTH, nqk), fixed), pl.BlockSpec((1, LANES), fixed),
                  pl.BlockSpec((LANES, GLA_HEADS * GLA_DK), fixed), pl.BlockSpec((1, GLA_HEADS * GLA_DK), fixed),
                  pl.BlockSpec((1, MLSTM_HEADS * MLSTM_DV), fixed), pl.BlockSpec((1, GLA_HEADS * GLA_DV), fixed)],
        out_specs=pl.BlockSpec((tb, MIX_WIDTH), lambda i: (i, 0)),
        out_shape=jax.ShapeDtypeStruct((t, MIX_WIDTH), BF16),
        scratch_shapes=[pltpu.VMEM((tb + 8, nqk), F32), pltpu.VMEM((tb, nqk), F32),
                        pltpu.VMEM((MLSTM_HEADS, MLSTM_DK, MLSTM_DV), F32), pltpu.VMEM((8, MLSTM_DK), F32),
                        pltpu.VMEM((8, LANES), F32), pltpu.VMEM((GLA_HEADS, GLA_DV, GLA_DK), F32)],
        compiler_params=_params(("arbitrary",)),
    )(big, small, convw, gbias, w2pad, glab, mng, gng)


def _rope_kernel(big_ref, small_ref, pos_ref, freq_ref, sign_ref, q_ref, k_ref, qi_ref, kit_ref, wi_ref):
    pos = pos_ref[...].astype(F32)
    lane = lax.broadcasted_iota(jnp.int32, (pos.shape[0], LANES), 1)

    def tables(row):
        ang = pos * freq_ref[row:row + 1, :]
        return jnp.cos(ang), jnp.sin(ang) * sign_ref[row:row + 1, :]

    def rotate(x, cos, sin, half, period):
        up = pltpu.roll(x, LANES - half, axis=1)
        dn = pltpu.roll(x, half, axis=1)
        return x * cos + jnp.where((lane % period) < half, up, dn) * sin

    cos_a, sin_a = tables(0)
    for h in range(ATTN_HEADS):
        x = big_ref[:, O_Q + h * ATTN_DH:O_Q + (h + 1) * ATTN_DH].astype(F32)
        r = rotate(x, cos_a, sin_a, ROPE_DIM // 2, ATTN_DH) * QK_SCALE_LOG2
        q_ref[:, h * ATTN_DH:(h + 1) * ATTN_DH] = r.astype(BF16)
    for h in range(ATTN_KV_HEADS):
        x = big_ref[:, O_K + h * ATTN_DH:O_K + (h + 1) * ATTN_DH].astype(F32)
        k_ref[:, h * ATTN_DH:(h + 1) * ATTN_DH] = rotate(x, cos_a, sin_a, ROPE_DIM // 2, ATTN_DH).astype(BF16)
    cos_i, sin_i = tables(1)
    for h in range(IDX_WIDTH // LANES):
        x = big_ref[:, O_QI + h * LANES:O_QI + (h + 1) * LANES].astype(F32)
        qi_ref[:, h * LANES:(h + 1) * LANES] = rotate(x, cos_i, sin_i, IDX_ROPE_DIM // 2, IDX_DH).astype(BF16)
    cos_k, sin_k = tables(2)
    sm = small_ref[...]
    ki = rotate(sm, cos_k, sin_k, IDX_ROPE_DIM // 2, IDX_DH)[:, :IDX_DH].astype(BF16)
    eye = (lax.broadcasted_iota(jnp.int32, (IDX_DH, IDX_DH), 0)
           == lax.broadcasted_iota(jnp.int32, (IDX_DH, IDX_DH), 1)).astype(BF16)
    kit_ref[...] = _dot_nt(eye, ki).astype(BF16)
    wi_ref[...] = pltpu.roll(sm, LANES - IDX_DH, axis=1) * (IDX_HEADS ** -0.5 * IDX_DH ** -0.5)


def _rope_tables():
    def pattern(rot_dim, period, width):
        half = rot_dim // 2
        inv = 1.0 / (ROPE_THETA ** (jnp.arange(half, dtype=F32) / half))
        l = np.arange(LANES)
        inside = ((l % period) < rot_dim) & (l < width)
        freq = jnp.where(inside, inv[(l % period) % half], 0.0)
        sign = np.where(inside, np.where((l % period) < half, -1.0, 1.0), 0.0)
        return freq, jnp.asarray(sign, F32)

    fa, sa = pattern(ROPE_DIM, ATTN_DH, LANES)
    fi, si = pattern(IDX_ROPE_DIM, IDX_DH, LANES)
    fk, sk = pattern(IDX_ROPE_DIM, IDX_DH, IDX_DH)
    z = jnp.zeros((5, LANES), F32)
    return jnp.concatenate([jnp.stack([fa, fi, fk]), z]), jnp.concatenate([jnp.stack([sa, si, sk]), z])


def _rope(big, small, pos, tm=512):
    t = big.shape[0]
    freq, sign = _rope_tables()
    row = lambda i: (i, 0)
    fixed = lambda i: (0, 0)
    return pl.pallas_call(
        _rope_kernel,
        grid=(t // tm,),
        in_specs=[pl.BlockSpec((tm, O_BIG), row), pl.BlockSpec((tm, LANES), row), pl.BlockSpec((tm, 1), row),
                  pl.BlockSpec((8, LANES), fixed), pl.BlockSpec((8, LANES), fixed)],
        out_specs=[pl.BlockSpec((tm, ATTN_WIDTH), row), pl.BlockSpec((tm, KV_WIDTH), row),
                   pl.BlockSpec((tm, IDX_WIDTH), row), pl.BlockSpec((IDX_DH, tm), lambda i: (0, i)),
                   pl.BlockSpec((tm, LANES), row)],
        out_shape=[jax.ShapeDtypeStruct((t, ATTN_WIDTH), BF16), jax.ShapeDtypeStruct((t, KV_WIDTH), BF16),
                   jax.ShapeDtypeStruct((t, IDX_WIDTH), BF16), jax.ShapeDtypeStruct((IDX_DH, t), BF16),
                   jax.ShapeDtypeStruct((t, LANES), F32)],
        compiler_params=_params(("parallel",)),
        name="rope",
    )(big, small, pos, freq, sign)


def _dsa_kernel(q_ref, qi_ref, wi_ref, kit_ref, k_ref, v_ref, o_ref,
                keys_s, bias_s, p_s, alpha_s, m_s, acc_s, *, tq, tk, top_k):
    qb = pl.program_id(0)
    n_tiles = (qb * tq + tq - 1) // tk + 1

    w = wi_ref[...]
    qpos = qb * tq + lax.broadcasted_iota(jnp.int32, (tq, tk), 0)
    kiota = lax.broadcasted_iota(jnp.int32, (tq, tk), 1)

    def score_tile(c, carry):
        c0 = pl.multiple_of(c * tk, tk)
        kt = kit_ref[:, pl.ds(c0, tk)]
        sc = jnp.zeros((tq, tk), F32)
        for h in range(IDX_HEADS):
            lg = jnp.dot(qi_ref[:, h * IDX_DH:(h + 1) * IDX_DH], kt, preferred_element_type=F32)
            sc = sc + jnp.maximum(lg, 0.0) * w[:, h:h + 1]
        bits = pltpu.bitcast(sc, jnp.int32)
        key = bits ^ ((bits >> 31) & 0x7FFFFFFF)
        keys_s[:, pl.ds(c0, tk)] = jnp.where(kiota + c0 <= qpos, key, INT_MIN)
        return carry

    lax.fori_loop(0, n_tiles, score_tile, 0)

    def count_ge(cand):
        def body(c, acc):
            c0 = pl.multiple_of(c * tk, tk)
            ge = (keys_s[:, pl.ds(c0, tk)] >= cand).astype(jnp.int32)
            for u in range(tk // LANES):
                acc = acc + ge[:, u * LANES:(u + 1) * LANES]
            return acc

        acc = lax.fori_loop(0, n_tiles, body, jnp.zeros((tq, LANES), jnp.int32))
        return jnp.sum(acc, axis=1, keepdims=True)

    zero = jnp.zeros((tq, 1), jnp.int32)
    cnt0 = count_ge(zero)
    nonneg = cnt0 >= top_k
    base0 = jnp.where(nonneg, zero, INT_MIN)
    kept0 = jnp.where(nonneg, cnt0, n_tiles * tk)

    def unsettled(state):
        b, _, kept = state
        return jnp.logical_and(b < 31, jnp.max(jnp.abs(kept - top_k)) > 0)

    def bit_step(state):
        b, base, kept = state
        cand = base | jnp.left_shift(jnp.int32(1), 30 - b)
        cnt = count_ge(cand)
        ok = cnt >= top_k
        return b + 1, jnp.where(ok, cand, base), jnp.where(ok, cnt, kept)

    _, thr, _ = lax.while_loop(unsettled, bit_step, (jnp.int32(0), base0, kept0))
    thr = jnp.maximum(thr, INT_MIN + 1)

    m_s[...] = jnp.full_like(m_s, NEG_BIG)
    acc_s[...] = jnp.zeros_like(acc_s)
    ones = jnp.ones((tk, ATTN_DH), BF16)

    def flash_tile(c, carry):
        c0 = pl.multiple_of(c * tk, tk)
        bias_s[...] = jnp.where(keys_s[:, pl.ds(c0, tk)] >= thr, 0.0, NEG_BIG)
        for g in range(ATTN_KV_HEADS):
            qg = jnp.concatenate(
                [q_ref[:, (g * Q_PER_KV + r) * ATTN_DH:(g * Q_PER_KV + r + 1) * ATTN_DH] for r in range(Q_PER_KV)],
                axis=0)
            s = _dot_nt(qg, k_ref[pl.ds(c0, tk), g * ATTN_DH:(g + 1) * ATTN_DH])
            for r in range(Q_PER_KV):
                rows = slice(r * tq, (r + 1) * tq)
                sr = s[rows, :] + bias_s[...]
                m_prev = m_s[g, rows, :]
                m_new = jnp.maximum(m_prev, jnp.max(sr, axis=1, keepdims=True))
                p_s[rows, :] = jnp.exp2(sr - jnp.tile(m_new, (1, tk // LANES))).astype(BF16)
                alpha_s[rows, :] = jnp.exp2(m_prev - m_new)
                m_s[g, rows, :] = m_new
            v_ext = jnp.concatenate([v_ref[pl.ds(c0, tk), g * ATTN_DH:(g + 1) * ATTN_DH], ones], axis=1)
            pv = jnp.dot(p_s[...], v_ext, preferred_element_type=F32)
            acc_s[g] = jnp.tile(alpha_s[...], (1, 2)) * acc_s[g] + pv
        return carry

    lax.fori_loop(0, n_tiles, flash_tile, 0)

    for g in range(ATTN_KV_HEADS):
        a = acc_s[g]
        res = a[:, :ATTN_DH] / a[:, ATTN_DH:]
        for r in range(Q_PER_KV):
            hcol = (g * Q_PER_KV + r) * ATTN_DH
            o_ref[:, hcol:hcol + ATTN_DH] = res[r * tq:(r + 1) * tq, :].astype(BF16)


def _dsa(q, qi, wi, kit, k, big, tq=128, tk=512):
    t = q.shape[0]
    top_k = min(INDEX_TOPK, t // 4)
    v_blk = O_V // KV_WIDTH
    qrow = lambda i: (i, 0)
    once = pl.Buffered(1)
    rq = Q_PER_KV * tq
    return pl.pallas_call(
        functools.partial(_dsa_kernel, tq=tq, tk=tk, top_k=top_k),
        grid=(t // tq,),
        in_specs=[pl.BlockSpec((tq, ATTN_WIDTH), qrow), pl.BlockSpec((tq, IDX_WIDTH), qrow),
                  pl.BlockSpec((tq, LANES), qrow),
                  pl.BlockSpec((IDX_DH, t), lambda i: (0, 0), pipeline_mode=once),
                  pl.BlockSpec((t, KV_WIDTH), lambda i: (0, 0), pipeline_mode=once),
                  pl.BlockSpec((t, KV_WIDTH), lambda i: (0, v_blk), pipeline_mode=once)],
        out_specs=pl.BlockSpec((tq, ATTN_WIDTH), qrow),
        out_shape=jax.ShapeDtypeStruct((t, ATTN_WIDTH), BF16),
        scratch_shapes=[pltpu.VMEM((tq, t), jnp.int32), pltpu.VMEM((tq, tk), F32), pltpu.VMEM((rq, tk), BF16),
                        pltpu.VMEM((rq, LANES), F32), pltpu.VMEM((ATTN_KV_HEADS, rq, LANES), F32),
                        pltpu.VMEM((ATTN_KV_HEADS, rq, 2 * ATTN_DH), F32)],
        compiler_params=pltpu.CompilerParams(dimension_semantics=("parallel",), vmem_limit_bytes=DSA_VMEM_LIMIT),
        name="dsa_attention",
    )(q, qi, wi, kit, k, big)


def _pad_cols(w, width):
    return jnp.pad(w, ((0, 0), (0, width - w.shape[1])))


def _even_weights(w_in, igate_b, fgate_b, gla_w2, gla_b):
    cuts = np.cumsum([512, 512, 1024, 4, 4, 1024, 512, 512, 1024, 16, 1024])[:-1]
    mq, mk, mv, mi, mf, mo, gq, gk, gv, glr, gr = jnp.split(w_in, [int(c) for c in cuts], axis=1)
    w_big = jnp.concatenate([mq, mk, mv, mo, gq, gk, gv, gr], axis=1).astype(BF16)
    w_small = _pad_cols(jnp.concatenate([mi, mf, glr], axis=1), LANES).astype(BF16)
    gbias = _pad_cols(jnp.concatenate([igate_b, fgate_b])[None, :], LANES)
    w2pad = jnp.zeros((LANES, GLA_HEADS * GLA_DK), F32).at[S_GLR:S_GLR + GLA_GATE_RANK].set(gla_w2)
    return w_big, w_small, gbias, w2pad, gla_b[None, :]


def _odd_weights(w_in):
    cuts = np.cumsum([ATTN_WIDTH, KV_WIDTH, KV_WIDTH, IDX_WIDTH, IDX_DH, IDX_HEADS])[:-1]
    q, k, v, qi, ki, wi = jnp.split(w_in, [int(c) for c in cuts], axis=1)
    w_big = jnp.concatenate([q, k, v, qi], axis=1).astype(BF16)
    w_small = _pad_cols(jnp.concatenate([ki, wi], axis=1), LANES).astype(BF16)
    return w_big, w_small


def _even_mixer(hb, w_in, conv_w, igate_b, fgate_b, gla_w2, gla_b, mng, gng):
    w_big, w_small, gbias, w2pad, glab = _even_weights(w_in, igate_b, fgate_b, gla_w2, gla_b)
    big = _matmul(hb, w_big, BF16, 1024, 1024)
    small = _matmul(hb, w_small, F32, 1024, LANES)
    return _mixer(big, small, conv_w, gbias, w2pad, glab, mng[None, :], gng[None, :])


def _odd_mixer(hb, pos, w_in):
    w_big, w_small = _odd_weights(w_in)
    big = _matmul(hb, w_big, BF16, 1024, 1024)
    small = _matmul(hb, w_small, F32, 1024, LANES)
    q, k, qi, kit, wi = _rope(big, small, pos)
    return _dsa(q, qi, wi, kit, k, big)


def kernel(x, positions, ab_w_in, ab_conv_w, ab_igate_b, ab_fgate_b, ab_gla_gate_w2, ab_gla_gate_b, ab_mlstm_norm_g, ab_gla_norm_g, ab_w_out, c_w_in, c_w_out, router_w, router_bias, moe_w_gate, moe_w_up, moe_w_down, ln_mix_g, ln_mix_b, ln_ffn_g, ln_ffn_b):
    bsz, t, d = x.shape
    outs = []
    rwt = router_w.T
    rbias = router_bias[:, None]
    flat = lambda w: w.reshape((DEPTH * N_EXPERTS,) + w.shape[2:])
    wg, wu, wd = _cast_expert_weights(flat(moe_w_gate), flat(moe_w_up), flat(moe_w_down))
    for bi in range(bsz):
        h = x[bi]
        hb = h.astype(BF16)
        pos = positions[bi][:, None]
        for layer in range(DEPTH):
            j = layer // 2
            if layer % 2 == 0:
                mix = _even_mixer(hb, ab_w_in[j], ab_conv_w[j], ab_igate_b[j], ab_fgate_b[j], ab_gla_gate_w2[j],
                                  ab_gla_gate_b[j], ab_mlstm_norm_g[j], ab_gla_norm_g[j])
                w_out = ab_w_out[j]
            else:
                mix = _odd_mixer(hb, pos, c_w_in[j])
                w_out = c_w_out[j]
            h, hb = _outproj_ln(mix, w_out.astype(BF16), h, ln_mix_g[layer], ln_mix_b[layer])
            grp, comb_t = _router(h, rwt, rbias)
            y = _moe(h, grp[0], comb_t.T, wg, wu, wd, layer * N_EXPERTS)
            h, hb = _residual_ln(h, y, ln_ffn_g[layer], ln_ffn_b[layer])
        outs.append(h)
    return jnp.stack(outs)
```

```python
import functools

import jax
import jax.numpy as jnp
import numpy as np
from jax import lax
from jax.experimental import pallas as pl
from jax.experimental.pallas import tpu as pltpu

F32 = jnp.float32
BF16 = jnp.bfloat16
HIGHEST = lax.Precision.HIGHEST

D_MODEL = 2048
DEPTH = 4
MLSTM_HEADS = 4
MLSTM_DK = 128
MLSTM_DV = 256
CONV_WIDTH = 4
GLA_HEADS = 4
GLA_DK = 128
GLA_DV = 256
GLA_GATE_RANK = 16
GLA_GATE_TAU = 16.0
CHUNK = 64
ATTN_HEADS = 16
ATTN_KV_HEADS = 4
ATTN_DH = 128
ROPE_DIM = ATTN_DH // 4
ROPE_THETA = 500000.0
IDX_HEADS = 16
IDX_DH = 64
IDX_ROPE_DIM = IDX_DH // 4
INDEX_TOPK = 256
N_EXPERTS = 16
N_GROUPS = 4
EXPERTS_PER_GROUP = N_EXPERTS // N_GROUPS
EXPERT_FF = 512
DEEPNORM_ALPHA = (2 * DEPTH) ** 0.25
LN_EPS = 1e-5
HEAD_NORM_EPS = 1e-6

LANES = 128
MIX_WIDTH = MLSTM_HEADS * MLSTM_DV + GLA_HEADS * GLA_DV
ATTN_WIDTH = ATTN_HEADS * ATTN_DH
KV_WIDTH = ATTN_KV_HEADS * ATTN_DH
IDX_WIDTH = IDX_HEADS * IDX_DH
Q_PER_KV = ATTN_HEADS // ATTN_KV_HEADS

E_MQ = 0
E_MK = E_MQ + MLSTM_HEADS * MLSTM_DK
E_MV = E_MK + MLSTM_HEADS * MLSTM_DK
E_MO = E_MV + MLSTM_HEADS * MLSTM_DV
E_GQ = E_MO + MLSTM_HEADS * MLSTM_DV
E_GK = E_GQ + GLA_HEADS * GLA_DK
E_GV = E_GK + GLA_HEADS * GLA_DK
E_GR = E_GV + GLA_HEADS * GLA_DV
E_BIG = E_GR + GLA_HEADS * GLA_DV
S_MI = 0
S_MF = S_MI + MLSTM_HEADS
S_GLR = S_MF + MLSTM_HEADS
O_Q = 0
O_K = O_Q + ATTN_WIDTH
O_V = O_K + KV_WIDTH
O_QI = O_V + KV_WIDTH
O_BIG = O_QI + IDX_WIDTH

VMEM_LIMIT = 56 * 1024 * 1024
DSA_VMEM_LIMIT = 60 * 1024 * 1024
QK_SCALE_LOG2 = ATTN_DH ** -0.5 * 1.4426950408889634
INT_MIN = -(2**31)
NEG_BIG = -1e30


def _params(sem):
    return pltpu.CompilerParams(dimension_semantics=sem, vmem_limit_bytes=VMEM_LIMIT)


def _mm_kernel(x_ref, w_ref, o_ref):
    o_ref[...] = jnp.dot(x_ref[...], w_ref[...], preferred_element_type=F32).astype(o_ref.dtype)


def _matmul(x, w, out_dtype, tm, tn):
    t, k = x.shape
    n = w.shape[1]
    return pl.pallas_call(
        _mm_kernel,
        grid=(t // tm, n // tn),
        in_specs=[pl.BlockSpec((tm, k), lambda i, j: (i, 0)), pl.BlockSpec((k, tn), lambda i, j: (0, j))],
        out_specs=pl.BlockSpec((tm, tn), lambda i, j: (i, j)),
        out_shape=jax.ShapeDtypeStruct((t, n), out_dtype),
        compiler_params=_params(("parallel", "parallel")),
    )(x, w)


def _layer_norm(z, g, b):
    mu = jnp.mean(z, axis=-1, keepdims=True)
    zc = z - mu
    var = jnp.mean(zc * zc, axis=-1, keepdims=True)
    return zc * lax.rsqrt(var + LN_EPS) * g + b


def _outproj_ln_kernel(mix_ref, w_ref, h_ref, g_ref, b_ref, of_ref, ob_ref):
    y = jnp.dot(mix_ref[...], w_ref[...], preferred_element_type=F32)
    out = _layer_norm(DEEPNORM_ALPHA * h_ref[...] + y, g_ref[...], b_ref[...])
    of_ref[...] = out
    ob_ref[...] = out.astype(BF16)


def _outproj_ln(mix, w, h, g, b, tm=256):
    t, k = mix.shape
    d = w.shape[1]
    row = lambda i: (i, 0)
    fixed = lambda i: (0, 0)
    return pl.pallas_call(
        _outproj_ln_kernel,
        grid=(t // tm,),
        in_specs=[pl.BlockSpec((tm, k), row), pl.BlockSpec((k, d), fixed), pl.BlockSpec((tm, d), row),
                  pl.BlockSpec((1, d), fixed), pl.BlockSpec((1, d), fixed)],
        out_specs=[pl.BlockSpec((tm, d), row), pl.BlockSpec((tm, d), row)],
        out_shape=[jax.ShapeDtypeStruct((t, d), F32), jax.ShapeDtypeStruct((t, d), BF16)],
        compiler_params=_params(("parallel",)),
    )(mix, w, h, g.reshape(1, d), b.reshape(1, d))


def _router_kernel(h_ref, rwt_ref, bias_ref, grp_ref, comb_ref):
    logits = lax.dot_general(rwt_ref[...], h_ref[...], (((1,), (1,)), ((), ())),
                             precision=HIGHEST, preferred_element_type=F32)
    aff = jax.nn.sigmoid(logits)
    sel = aff + bias_ref[...]
    srow = [sel[e:e + 1, :] for e in range(N_EXPERTS)]
    arow = [aff[e:e + 1, :] for e in range(N_EXPERTS)]

    def top2_sum(v):
        best = v[0] + v[1]
        for a in range(len(v)):
            for b in range(a + 1, len(v)):
                if (a, b) != (0, 1):
                    best = jnp.maximum(best, v[a] + v[b])
        return best

    gscore = [top2_sum(srow[g * EXPERTS_PER_GROUP:(g + 1) * EXPERTS_PER_GROUP]) for g in range(N_GROUPS)]
    best = gscore[0]
    gi = jnp.zeros_like(best, dtype=jnp.int32)
    for g in range(1, N_GROUPS):
        upd = gscore[g] > best
        gi = jnp.where(upd, g, gi)
        best = jnp.where(upd, gscore[g], best)

    def pick(rows, k):
        out = rows[k]
        for g in range(1, N_GROUPS):
            out = jnp.where(gi == g, rows[g * EXPERTS_PER_GROUP + k], out)
        return out

    v = [pick(srow, k) for k in range(EXPERTS_PER_GROUP)]
    a = [pick(arow, k) for k in range(EXPERTS_PER_GROUP)]

    def first_argmax(vals):
        m = vals[0]
        for x in vals[1:]:
            m = jnp.maximum(m, x)
        loc = jnp.full_like(gi, len(vals) - 1)
        for k in range(len(vals) - 2, -1, -1):
            loc = jnp.where(vals[k] == m, k, loc)
        return loc

    loc1 = first_argmax(v)
    v2 = [jnp.where(loc1 == k, -jnp.inf, v[k]) for k in range(EXPERTS_PER_GROUP)]
    loc2 = first_argmax(v2)

    def take(vals, loc):
        out = vals[0]
        for k in range(1, len(vals)):
            out = jnp.where(loc == k, vals[k], out)
        return out

    g1 = take(a, loc1)
    g2 = take(a, loc2)
    tot = g1 + g2
    g1 = g1 / tot
    g2 = g2 / tot
    grp_ref[...] = gi
    krow = lax.broadcasted_iota(jnp.int32, comb_ref.shape, 0)
    comb_ref[...] = jnp.where(krow == loc1, g1, 0.0) + jnp.where(krow == loc2, g2, 0.0)


def _router(h, rwt, bias, tm=512):
    t, d = h.shape
    return pl.pallas_call(
        _router_kernel,
        grid=(t // tm,),
        in_specs=[pl.BlockSpec((tm, d), lambda i: (i, 0)), pl.BlockSpec((N_EXPERTS, d), lambda i: (0, 0)),
                  pl.BlockSpec((N_EXPERTS, 1), lambda i: (0, 0))],
        out_specs=[pl.BlockSpec((1, tm), lambda i: (0, i)), pl.BlockSpec((EXPERTS_PER_GROUP, tm), lambda i: (0, i))],
        out_shape=[jax.ShapeDtypeStruct((1, t), jnp.int32), jax.ShapeDtypeStruct((EXPERTS_PER_GROUP, t), F32)],
        compiler_params=_params(("parallel",)),
        name="router",
    )(h, rwt, bias)


def _cast_kernel(a_ref, b_ref, c_ref, ao_ref, bo_ref, co_ref):
    ao_ref[...] = a_ref[...].astype(BF16)
    bo_ref[...] = b_ref[...].astype(BF16)
    co_ref[...] = c_ref[...].astype(BF16)


def _cast_expert_weights(wg, wu, wd):
    n = wg.shape[0]
    spec = lambda a: pl.BlockSpec((1,) + a.shape[1:], lambda i: (i, 0, 0))
    return pl.pallas_call(
        _cast_kernel,
        grid=(n,),
        in_specs=[spec(wg), spec(wu), spec(wd)],
        out_specs=[spec(wg), spec(wu), spec(wd)],
        out_shape=[jax.ShapeDtypeStruct(a.shape, BF16) for a in (wg, wu, wd)],
        compiler_params=_params(("parallel",)),
        name="cast_expert_weights",
    )(wg, wu, wd)


def _moe_kernel(tg_ref, src_ref, nv_ref, nu_ref, h_hbm, comb_ref, wg_ref, wu_ref, wd_ref, y_hbm,
                xbuf, ybuf, gsem, ssem, *, tm):
    i = pl.program_id(0)
    e = pl.program_id(1)
    n_used = nu_ref[0]
    slot = lax.rem(i, 2)
    part = tm // EXPERTS_PER_GROUP

    def row_copy(r, tile):
        return pltpu.make_async_copy(ybuf.at[pl.ds(r, 1), :], y_hbm.at[pl.ds(src_ref[tile * tm + r], 1), :],
                                     ssem.at[0])

    def gather_rows(tile, dst_slot, lo, hi):
        def body(r, carry):
            tok = src_ref[tile * tm + r]
            pltpu.make_async_copy(h_hbm.at[pl.ds(tok, 1), :], xbuf.at[dst_slot, pl.ds(r, 1), :],
                                  gsem.at[dst_slot]).start()
            return carry

        lax.fori_loop(lo, hi, body, 0)

    def wait_gather(s):
        pltpu.make_async_copy(xbuf.at[s], xbuf.at[s], gsem.at[s]).wait()

    def wait_scatter(tile):
        def body(r, carry):
            row_copy(r, tile).wait()
            return carry

        lax.fori_loop(0, nv_ref[tile], body, 0)

    @pl.when(i < n_used)
    def _():
        @pl.when(jnp.logical_and(i == 0, e == 0))
        def _():
            gather_rows(0, 0, 0, tm)

        @pl.when(e == 0)
        def _():
            wait_gather(slot)

        nxt = jnp.minimum(i + 1, pl.num_programs(0) - 1)
        for r in range(part):
            row = e * part + r
            pltpu.make_async_copy(h_hbm.at[pl.ds(src_ref[nxt * tm + row], 1), :],
                                  xbuf.at[1 - slot, pl.ds(row, 1), :], gsem.at[1 - slot]).start()

        comb = comb_ref[...]
        lane = lax.broadcasted_iota(jnp.int32, comb.shape, 1)
        c = jnp.sum(jnp.where(lane == e, comb, 0.0), axis=1, keepdims=True)
        x = xbuf[slot].astype(BF16)
        hg = jnp.dot(x, wg_ref[0], preferred_element_type=F32)
        hu = jnp.dot(x, wu_ref[0], preferred_element_type=F32)
        hh = (hg * jax.nn.sigmoid(hg) * hu) * c
        contrib = jnp.dot(hh.astype(BF16), wd_ref[0], preferred_element_type=F32)

        @pl.when(e == 0)
        def _():
            @pl.when(i > 0)
            def _():
                wait_scatter(i - 1)

            ybuf[...] = contrib

        @pl.when(e > 0)
        def _():
            ybuf[...] += contrib

        @pl.when(e == EXPERTS_PER_GROUP - 1)
        def _():
            def body(r, carry):
                row_copy(r, i).start()
                return carry

            lax.fori_loop(0, nv_ref[i], body, 0)

            @pl.when(i == n_used - 1)
            def _():
                wait_scatter(i)
                wait_gather(1 - slot)


def _moe(h, grp, comb, wg, wu, wd, first_expert, tm=512):
    t, d = h.shape
    f = wg.shape[-1]
    nt = t // tm + N_GROUPS
    p = nt * tm
    onehot = (grp[:, None] == jnp.arange(N_GROUPS)[None, :]).astype(jnp.int32)
    csum = jnp.cumsum(onehot, axis=0)
    counts = csum[-1]
    rank = jnp.sum(csum * onehot, axis=1) - 1
    padded = (counts + tm - 1) // tm * tm
    ends = jnp.cumsum(padded)
    starts = ends - padded
    pos = jnp.sum(starts[None, :] * onehot, axis=1) + rank
    payload = jnp.concatenate([jnp.arange(t, dtype=jnp.int32)[:, None], lax.bitcast_convert_type(comb, jnp.int32)],
                              axis=1)
    table = jnp.zeros((p, 1 + EXPERTS_PER_GROUP), jnp.int32).at[pos].set(payload)
    src = table[:, 0]
    comb_s = lax.bitcast_convert_type(table[:, 1:], F32)
    tile_row0 = jnp.arange(nt, dtype=jnp.int32) * tm
    tile_group = jnp.minimum(jnp.sum((tile_row0[:, None] >= ends[None, :]).astype(jnp.int32), axis=1), N_GROUPS - 1)
    n_valid = jnp.clip(starts[tile_group] + counts[tile_group] - tile_row0, 0, tm).astype(jnp.int32)
    n_used = (ends[-1] // tm).astype(jnp.int32).reshape(1)

    wmap = lambda i, e, tg, s, nv, nu: (first_expert + tg[i] * EXPERTS_PER_GROUP + e, 0, 0)
    grid_spec = pltpu.PrefetchScalarGridSpec(
        num_scalar_prefetch=4,
        grid=(nt, EXPERTS_PER_GROUP),
        in_specs=[pl.BlockSpec(memory_space=pl.ANY),
                  pl.BlockSpec((tm, EXPERTS_PER_GROUP), lambda i, e, tg, s, nv, nu: (i, 0)),
                  pl.BlockSpec((1, d, f), wmap), pl.BlockSpec((1, d, f), wmap), pl.BlockSpec((1, f, d), wmap)],
        out_specs=pl.BlockSpec(memory_space=pl.ANY),
        scratch_shapes=[pltpu.VMEM((2, tm, d), F32), pltpu.VMEM((tm, d), F32),
                        pltpu.SemaphoreType.DMA((2,)), pltpu.SemaphoreType.DMA((1,))],
    )
    return pl.pallas_call(
        functools.partial(_moe_kernel, tm=tm),
        grid_spec=grid_spec,
        out_shape=jax.ShapeDtypeStruct((t, d), F32),
        compiler_params=_params(("arbitrary", "arbitrary")),
        name="moe_grouped",
    )(tile_group, src, n_valid, n_used, h, comb_s, wg, wu, wd)


def _residual_ln_kernel(h_ref, y_ref, g_ref, b_ref, of_ref, ob_ref):
    out = _layer_norm(DEEPNORM_ALPHA * h_ref[...] + y_ref[...], g_ref[...], b_ref[...])
    of_ref[...] = out
    ob_ref[...] = out.astype(BF16)


def _residual_ln(h, y, g, b, tm=512):
    t, d = h.shape
    row = lambda i: (i, 0)
    fixed = lambda i: (0, 0)
    return pl.pallas_call(
        _residual_ln_kernel,
        grid=(t // tm,),
        in_specs=[pl.BlockSpec((tm, d), row), pl.BlockSpec((tm, d), row), pl.BlockSpec((1, d), fixed),
                  pl.BlockSpec((1, d), fixed)],
        out_specs=[pl.BlockSpec((tm, d), row), pl.BlockSpec((tm, d), row)],
        out_shape=[jax.ShapeDtypeStruct((t, d), F32), jax.ShapeDtypeStruct((t, d), BF16)],
        compiler_params=_params(("parallel",)),
        name="residual_ln",
    )(h, y, g.reshape(1, d), b.reshape(1, d))


def _log_sigmoid(x):
    return jnp.minimum(x, 0.0) - jnp.log(1.0 + jnp.exp(-jnp.abs(x)))


def _dot_nt(a, b):
    return lax.dot_general(a, b, (((1,), (1,)), ((), ())), preferred_element_type=F32)


def _dot_tn(a, b, precision=None):
    return lax.dot_general(a, b, (((0,), (0,)), ((), ())), precision=precision, preferred_element_type=F32)


def _head_rms(x, gain):
    return x * lax.rsqrt(jnp.mean(x * x, axis=1, keepdims=True) + HEAD_NORM_EPS) * gain


def _mixer_kernel(big_ref, small_ref, convw_ref, gbias_ref, w2_ref, glab_ref, mng_ref, gng_ref, o_ref,
                  ext_s, qk_s, c_s, n_s, m_s, s_s, *, tb):
    i = pl.program_id(0)
    nqk = 2 * MLSTM_HEADS * MLSTM_DK
    tail = CONV_WIDTH - 1

    @pl.when(i == 0)
    def _():
        ext_s[0:8, :] = jnp.zeros((8, nqk), F32)
        c_s[...] = jnp.zeros_like(c_s)
        n_s[...] = jnp.zeros_like(n_s)
        m_s[...] = jnp.zeros_like(m_s)
        s_s[...] = jnp.zeros_like(s_s)

    ext_s[8:8 + tb, :] = big_ref[:, E_MQ:E_MQ + nqk].astype(F32)
    conv = ext_s[8:8 + tb, :] * convw_ref[tail:tail + 1, :]
    for j in range(tail):
        conv = conv + ext_s[8 - tail + j:8 - tail + j + tb, :] * convw_ref[j:j + 1, :]
    qk_s[...] = conv * jax.nn.sigmoid(conv)
    ext_s[0:8, :] = ext_s[tb:tb + 8, :]

    rows = lax.broadcasted_iota(jnp.int32, (CHUNK, CHUNK), 0)
    cols = lax.broadcasted_iota(jnp.int32, (CHUNK, CHUNK), 1)
    causal = rows >= cols
    ltri = causal.astype(F32)
    eye = (rows == cols).astype(F32)
    lane = lax.broadcasted_iota(jnp.int32, (CHUNK, LANES), 1)
    is_f = (lane >= S_MF) & (lane < S_MF + MLSTM_HEADS)

    def chunk(c, carry):
        r0 = pl.multiple_of(c * CHUNK, CHUNK)
        rs = pl.ds(r0, CHUNK)
        gates = small_ref[rs, :]
        pre = gates + gbias_ref[...]
        xg = jnp.where(is_f, _log_sigmoid(pre), pre)
        cum = jnp.dot(ltri, xg, precision=HIGHEST, preferred_element_type=F32)
        y = jnp.where(is_f, cum, xg)
        yt = _dot_tn(y, eye, precision=HIGHEST)

        for h in range(MLSTM_HEADS):
            q = qk_s[rs, h * MLSTM_DK:(h + 1) * MLSTM_DK]
            k = qk_s[rs, (MLSTM_HEADS + h) * MLSTM_DK:(MLSTM_HEADS + h + 1) * MLSTM_DK] * MLSTM_DK ** -0.5
            v = big_ref[rs, E_MV + h * MLSTM_DV:E_MV + (h + 1) * MLSTM_DV]
            ig_c = y[:, S_MI + h:S_MI + h + 1]
            b_c = y[:, S_MF + h:S_MF + h + 1]
            ig_r = yt[S_MI + h:S_MI + h + 1, :]
            b_r = yt[S_MF + h:S_MF + h + 1, :]
            b_last = y[CHUNK - 1:CHUNK, S_MF + h:S_MF + h + 1]
            c_prev = c_s[h]
            n_prev = n_s[h:h + 1, :]
            m_prev = m_s[h:h + 1, 0:1]
            d_log = jnp.where(causal, b_c - b_r + ig_r, -jnp.inf)
            inter = b_c + m_prev
            m_q = jnp.maximum(inter, jnp.max(d_log, axis=1, keepdims=True))
            qb = q.astype(BF16)
            w = _dot_nt(qb, k.astype(BF16)) * jnp.exp(d_log - m_q)
            s_int = jnp.exp(inter - m_q)
            num = jnp.dot(w.astype(BF16), v, preferred_element_type=F32) + s_int * jnp.dot(
                qb, c_prev.astype(BF16), preferred_element_type=F32)
            den = jnp.sum(w, axis=1, keepdims=True) + s_int * jnp.sum(q * n_prev, axis=1, keepdims=True)
            out = num / jnp.maximum(jnp.abs(den), jnp.exp(-m_q))
            a_c = b_last - b_c + ig_c
            m_loc = jnp.max(a_c, axis=0, keepdims=True)
            kw = k * jnp.exp(a_c - m_loc)
            c_loc = _dot_tn(kw.astype(BF16), v)
            n_loc = jnp.sum(kw, axis=0, keepdims=True)
            m_new = jnp.maximum(b_last + m_prev, m_loc)
            s_old = jnp.exp(b_last + m_prev - m_new)
            s_loc = jnp.exp(m_loc - m_new)
            c_s[h] = s_old * c_prev + s_loc * c_loc
            n_s[h:h + 1, :] = s_old * n_prev + s_loc * n_loc
            m_s[h:h + 1, :] = jnp.broadcast_to(m_new, (1, LANES))
            hn = _head_rms(out, mng_ref[:, h * MLSTM_DV:(h + 1) * MLSTM_DV])
            og = big_ref[rs, E_MO + h * MLSTM_DV:E_MO + (h + 1) * MLSTM_DV].astype(F32)
            o_ref[rs, h * MLSTM_DV:(h + 1) * MLSTM_DV] = (hn * jax.nn.sigmoid(og)).astype(BF16)

        z = jnp.dot(gates, w2_ref[...], precision=HIGHEST, preferred_element_type=F32) + glab_ref[...]
        log_a = _log_sigmoid(z) / GLA_GATE_TAU
        la_all = jnp.dot(ltri, log_a, precision=HIGHEST, preferred_element_type=F32)
        for h in range(GLA_HEADS):
            la = la_all[:, h * GLA_DK:(h + 1) * GLA_DK]
            la_last = la[CHUNK - 1:CHUNK, :]
            q = big_ref[rs, E_GQ + h * GLA_DK:E_GQ + (h + 1) * GLA_DK].astype(F32) * GLA_DK ** -0.5
            k = big_ref[rs, E_GK + h * GLA_DK:E_GK + (h + 1) * GLA_DK].astype(F32)
            v = big_ref[rs, E_GV + h * GLA_DV:E_GV + (h + 1) * GLA_DV]
            q_dec = (q * jnp.exp(la)).astype(BF16)
            k_dec = (k * jnp.exp(-la)).astype(BF16)
            k_end = (k * jnp.exp(la_last - la)).astype(BF16)
            attn = jnp.where(causal, _dot_nt(q_dec, k_dec), 0.0)
            st_prev = s_s[h]
            out = jnp.dot(attn.astype(BF16), v, preferred_element_type=F32) + _dot_nt(q_dec, st_prev.astype(BF16))
            s_s[h] = jnp.exp(la_last) * st_prev + _dot_tn(v, k_end)
            hn = _head_rms(out, gng_ref[:, h * GLA_DV:(h + 1) * GLA_DV])
            gr = big_ref[rs, E_GR + h * GLA_DV:E_GR + (h + 1) * GLA_DV].astype(F32)
            col = MLSTM_HEADS * MLSTM_DV + h * GLA_DV
            o_ref[rs, col:col + GLA_DV] = (hn * (gr * jax.nn.sigmoid(gr))).astype(BF16)
        return carry

    lax.fori_loop(0, tb // CHUNK, chunk, 0)


def _mixer(big, small, convw, gbias, w2pad, glab, mng, gng, tb=512):
    t = big.shape[0]
    nqk = 2 * MLSTM_HEADS * MLSTM_DK
    fixed = lambda i: (0, 0)
    return pl.pallas_call(
        functools.partial(_mixer_kernel, tb=tb),
        grid=(t // tb,),
        in_specs=[pl.BlockSpec((tb, E_BIG), lambda i: (i, 0)), pl.BlockSpec((tb, LANES), lambda i: (i, 0)),
                  pl.BlockSpec((CONV_WIDTH, nqk), fixed), pl.BlockSpec((1, LANES), fixed),
                  pl.BlockSpec((LANES, GLA_HEADS * GLA_DK), fixed), pl.BlockSpec((1, GLA_HEADS * GLA_DK), fixed),
                  pl.BlockSpec((1, MLSTM_HEADS * MLSTM_DV), fixed), pl.BlockSpec((1, GLA_HEADS * GLA_DV), fixed)],
        out_specs=pl.BlockSpec((tb, MIX_WIDTH), lambda i: (i, 0)),
        out_shape=jax.ShapeDtypeStruct((t, MIX_WIDTH), BF16),
        scratch_shapes=[pltpu.VMEM((tb + 8, nqk), F32), pltpu.VMEM((tb, nqk), F32),
                        pltpu.VMEM((MLSTM_HEADS, MLSTM_DK, MLSTM_DV), F32), pltpu.VMEM((8, MLSTM_DK), F32),
                        pltpu.VMEM((8, LANES), F32), pltpu.VMEM((GLA_HEADS, GLA_DV, GLA_DK), F32)],
        compiler_params=_params(("arbitrary",)),
    )(big, small, convw, gbias, w2pad, glab, mng, gng)


def _rope_kernel(big_ref, small_ref, pos_ref, freq_ref, sign_ref, q_ref, k_ref, qi_ref, kit_ref, wi_ref):
    pos = pos_ref[...].astype(F32)
    lane = lax.broadcasted_iota(jnp.int32, (pos.shape[0], LANES), 1)

    def tables(row):
        ang = pos * freq_ref[row:row + 1, :]
        return jnp.cos(ang), jnp.sin(ang) * sign_ref[row:row + 1, :]

    def rotate(x, cos, sin, half, period):
        up = pltpu.roll(x, LANES - half, axis=1)
        dn = pltpu.roll(x, half, axis=1)
        return x * cos + jnp.where((lane % period) < half, up, dn) * sin

    cos_a, sin_a = tables(0)
    for h in range(ATTN_HEADS):
        x = big_ref[:, O_Q + h * ATTN_DH:O_Q + (h + 1) * ATTN_DH].astype(F32)
        r = rotate(x, cos_a, sin_a, ROPE_DIM // 2, ATTN_DH) * QK_SCALE_LOG2
        q_ref[:, h * ATTN_DH:(h + 1) * ATTN_DH] = r.astype(BF16)
    for h in range(ATTN_KV_HEADS):
        x = big_ref[:, O_K + h * ATTN_DH:O_K + (h + 1) * ATTN_DH].astype(F32)
        k_ref[:, h * ATTN_DH:(h + 1) * ATTN_DH] = rotate(x, cos_a, sin_a, ROPE_DIM // 2, ATTN_DH).astype(BF16)
    cos_i, sin_i = tables(1)
    for h in range(IDX_WIDTH // LANES):
        x = big_ref[:, O_QI + h * LANES:O_QI + (h + 1) * LANES].astype(F32)
        qi_ref[:, h * LANES:(h + 1) * LANES] = rotate(x, cos_i, sin_i, IDX_ROPE_DIM // 2, IDX_DH).astype(BF16)
    cos_k, sin_k = tables(2)
    sm = small_ref[...]
    ki = rotate(sm, cos_k, sin_k, IDX_ROPE_DIM // 2, IDX_DH)[:, :IDX_DH].astype(BF16)
    eye = (lax.broadcasted_iota(jnp.int32, (IDX_DH, IDX_DH), 0)
           == lax.broadcasted_iota(jnp.int32, (IDX_DH, IDX_DH), 1)).astype(BF16)
    kit_ref[...] = _dot_nt(eye, ki).astype(BF16)
    wi_ref[...] = pltpu.roll(sm, LANES - IDX_DH, axis=1) * (IDX_HEADS ** -0.5 * IDX_DH ** -0.5)


def _rope_tables():
    def pattern(rot_dim, period, width):
        half = rot_dim // 2
        inv = 1.0 / (ROPE_THETA ** (jnp.arange(half, dtype=F32) / half))
        l = np.arange(LANES)
        inside = ((l % period) < rot_dim) & (l < width)
        freq = jnp.where(inside, inv[(l % period) % half], 0.0)
        sign = np.where(inside, np.where((l % period) < half, -1.0, 1.0), 0.0)
        return freq, jnp.asarray(sign, F32)

    fa, sa = pattern(ROPE_DIM, ATTN_DH, LANES)
    fi, si = pattern(IDX_ROPE_DIM, IDX_DH, LANES)
    fk, sk = pattern(IDX_ROPE_DIM, IDX_DH, IDX_DH)
    z = jnp.zeros((5, LANES), F32)
    return jnp.concatenate([jnp.stack([fa, fi, fk]), z]), jnp.concatenate([jnp.stack([sa, si, sk]), z])


def _rope(big, small, pos, tm=512):
    t = big.shape[0]
    freq, sign = _rope_tables()
    row = lambda i: (i, 0)
    fixed = lambda i: (0, 0)
    return pl.pallas_call(
        _rope_kernel,
        grid=(t // tm,),
        in_specs=[pl.BlockSpec((tm, O_BIG), row), pl.BlockSpec((tm, LANES), row), pl.BlockSpec((tm, 1), row),
                  pl.BlockSpec((8, LANES), fixed), pl.BlockSpec((8, LANES), fixed)],
        out_specs=[pl.BlockSpec((tm, ATTN_WIDTH), row), pl.BlockSpec((tm, KV_WIDTH), row),
                   pl.BlockSpec((tm, IDX_WIDTH), row), pl.BlockSpec((IDX_DH, tm), lambda i: (0, i)),
                   pl.BlockSpec((tm, LANES), row)],
        out_shape=[jax.ShapeDtypeStruct((t, ATTN_WIDTH), BF16), jax.ShapeDtypeStruct((t, KV_WIDTH), BF16),
                   jax.ShapeDtypeStruct((t, IDX_WIDTH), BF16), jax.ShapeDtypeStruct((IDX_DH, t), BF16),
                   jax.ShapeDtypeStruct((t, LANES), F32)],
        compiler_params=_params(("parallel",)),
        name="rope",
    )(big, small, pos, freq, sign)


def _dsa_kernel(q_ref, qi_ref, wi_ref, kit_ref, k_ref, v_ref, o_ref,
                keys_s, hi_s, bias_s, p_s, alpha_s, m_s, acc_s, *, tq, tk, top_k):
    qb = pl.program_id(0)
    n_tiles = (qb * tq + tq - 1) // tk + 1

    w = wi_ref[...]
    qpos = qb * tq + lax.broadcasted_iota(jnp.int32, (tq, tk), 0)
    kiota = lax.broadcasted_iota(jnp.int32, (tq, tk), 1)

    def score_tile(c, carry):
        c0 = pl.multiple_of(c * tk, tk)
        kt = kit_ref[:, pl.ds(c0, tk)]
        sc = jnp.zeros((tq, tk), F32)
        for h in range(IDX_HEADS):
            lg = jnp.dot(qi_ref[:, h * IDX_DH:(h + 1) * IDX_DH], kt, preferred_element_type=F32)
            sc = sc + jnp.maximum(lg, 0.0) * w[:, h:h + 1]
        bits = pltpu.bitcast(sc, jnp.int32)
        key = bits ^ ((bits >> 31) & 0x7FFFFFFF)
        key = jnp.where(kiota + c0 <= qpos, key, INT_MIN)
        keys_s[:, pl.ds(c0, tk)] = key
        hi_s[:, pl.ds(c0, tk)] = (key >> 16).astype(jnp.int16)
        return carry

    lax.fori_loop(0, n_tiles, score_tile, 0)

    def count_ge(cand):
        def body(c, acc):
            c0 = pl.multiple_of(c * tk, tk)
            ge = (keys_s[:, pl.ds(c0, tk)] >= cand).astype(jnp.int32)
            for u in range(tk // LANES):
                acc = acc + ge[:, u * LANES:(u + 1) * LANES]
            return acc

        acc = lax.fori_loop(0, n_tiles, body, jnp.zeros((tq, LANES), jnp.int32))
        return jnp.sum(acc, axis=1, keepdims=True)

    def count_hi_ge(cand):
        c16 = jnp.broadcast_to(cand, (tq, LANES)).astype(jnp.int16)
        one = jnp.ones((tq, LANES), BF16)
        nil = jnp.zeros((tq, LANES), BF16)

        def body(c, acc):
            c0 = pl.multiple_of(c * tk, tk)
            blk = hi_s[:, pl.ds(c0, tk)]
            for u in range(tk // LANES):
                acc = acc + jnp.where(blk[:, u * LANES:(u + 1) * LANES] >= c16, one, nil)
            return acc

        acc = lax.fori_loop(0, n_tiles, body, nil)
        return jnp.sum(acc.astype(F32), axis=1, keepdims=True).astype(jnp.int32)

    def search_bits(count_fn, base, kept, nbits):
        def unsettled(state):
            b, _, kept = state
            return jnp.logical_and(b < nbits, jnp.max(jnp.abs(kept - top_k)) > 0)

        def bit_step(state):
            b, base, kept = state
            cand = base | jnp.left_shift(jnp.int32(1), nbits - 1 - b)
            cnt = count_fn(cand)
            ok = cnt >= top_k
            return b + 1, jnp.where(ok, cand, base), jnp.where(ok, cnt, kept)

        _, base, kept = lax.while_loop(unsettled, bit_step, (jnp.int32(0), base, kept))
        return base, kept

    zero = jnp.zeros((tq, 1), jnp.int32)
    cnt0 = count_hi_ge(zero)
    nonneg = cnt0 >= top_k
    base = jnp.where(nonneg, zero, -(2 ** 15))
    kept = jnp.where(nonneg, cnt0, n_tiles * tk)
    base, kept = search_bits(count_hi_ge, base, kept, 15)
    thr, _ = search_bits(count_ge, jnp.left_shift(base, 16), kept, 16)
    thr = jnp.maximum(thr, INT_MIN + 1)

    m_s[...] = jnp.full_like(m_s, NEG_BIG)
    acc_s[...] = jnp.zeros_like(acc_s)
    ones = jnp.ones((tk, ATTN_DH), BF16)

    def flash_tile(c, carry):
        c0 = pl.multiple_of(c * tk, tk)
        bias_s[...] = jnp.where(keys_s[:, pl.ds(c0, tk)] >= thr, 0.0, NEG_BIG)
        for g in range(ATTN_KV_HEADS):
            qg = jnp.concatenate(
                [q_ref[:, (g * Q_PER_KV + r) * ATTN_DH:(g * Q_PER_KV + r + 1) * ATTN_DH] for r in range(Q_PER_KV)],
                axis=0)
            s = _dot_nt(qg, k_ref[pl.ds(c0, tk), g * ATTN_DH:(g + 1) * ATTN_DH])
            for r in range(Q_PER_KV):
                rows = slice(r * tq, (r + 1) * tq)
                sr = s[rows, :] + bias_s[...]
                m_prev = m_s[g, rows, :]
                m_new = jnp.maximum(m_prev, jnp.max(sr, axis=1, keepdims=True))
                p_s[rows, :] = jnp.exp2(sr - jnp.tile(m_new, (1, tk // LANES))).astype(BF16)
                alpha_s[rows, :] = jnp.exp2(m_prev - m_new)
                m_s[g, rows, :] = m_new
            v_ext = jnp.concatenate([v_ref[pl.ds(c0, tk), g * ATTN_DH:(g + 1) * ATTN_DH], ones], axis=1)
            pv = jnp.dot(p_s[...], v_ext, preferred_element_type=F32)
            acc_s[g] = jnp.tile(alpha_s[...], (1, 2)) * acc_s[g] + pv
        return carry

    lax.fori_loop(0, n_tiles, flash_tile, 0)

    for g in range(ATTN_KV_HEADS):
        a = acc_s[g]
        res = a[:, :ATTN_DH] / a[:, ATTN_DH:]
        for r in range(Q_PER_KV):
            hcol = (g * Q_PER_KV + r) * ATTN_DH
            o_ref[:, hcol:hcol + ATTN_DH] = res[r * tq:(r + 1) * tq, :].astype(BF16)


def _dsa(q, qi, wi, kit, k, big, tq=128, tk=512):
    t = q.shape[0]
    assert t // LANES <= 256, "per-lane key counts are accumulated in bf16"
    top_k = min(INDEX_TOPK, t // 4)
    v_blk = O_V // KV_WIDTH
    qrow = lambda i: (i, 0)
    once = pl.Buffered(1)
    rq = Q_PER_KV * tq
    return pl.pallas_call(
        functools.partial(_dsa_kernel, tq=tq, tk=tk, top_k=top_k),
        grid=(t // tq,),
        in_specs=[pl.BlockSpec((tq, ATTN_WIDTH), qrow), pl.BlockSpec((tq, IDX_WIDTH), qrow),
                  pl.BlockSpec((tq, LANES), qrow),
                  pl.BlockSpec((IDX_DH, t), lambda i: (0, 0), pipeline_mode=once),
                  pl.BlockSpec((t, KV_WIDTH), lambda i: (0, 0), pipeline_mode=once),
                  pl.BlockSpec((t, KV_WIDTH), lambda i: (0, v_blk), pipeline_mode=once)],
        out_specs=pl.BlockSpec((tq, ATTN_WIDTH), qrow),
        out_shape=jax.ShapeDtypeStruct((t, ATTN_WIDTH), BF16),
        scratch_shapes=[pltpu.VMEM((tq, t), jnp.int32), pltpu.VMEM((tq, t), jnp.int16),
                        pltpu.VMEM((tq, tk), F32), pltpu.VMEM((rq, tk), BF16),
                        pltpu.VMEM((rq, LANES), F32), pltpu.VMEM((ATTN_KV_HEADS, rq, LANES), F32),
                        pltpu.VMEM((ATTN_KV_HEADS, rq, 2 * ATTN_DH), F32)],
        compiler_params=pltpu.CompilerParams(dimension_semantics=("parallel",), vmem_limit_bytes=DSA_VMEM_LIMIT),
        name="dsa_attention",
    )(q, qi, wi, kit, k, big)


def _pad_cols(w, width):
    return jnp.pad(w, ((0, 0), (0, width - w.shape[1])))


def _even_weights(w_in, igate_b, fgate_b, gla_w2, gla_b):
    cuts = np.cumsum([512, 512, 1024, 4, 4, 1024, 512, 512, 1024, 16, 1024])[:-1]
    mq, mk, mv, mi, mf, mo, gq, gk, gv, glr, gr = jnp.split(w_in, [int(c) for c in cuts], axis=1)
    w_big = jnp.concatenate([mq, mk, mv, mo, gq, gk, gv, gr], axis=1).astype(BF16)
    w_small = _pad_cols(jnp.concatenate([mi, mf, glr], axis=1), LANES).astype(BF16)
    gbias = _pad_cols(jnp.concatenate([igate_b, fgate_b])[None, :], LANES)
    w2pad = jnp.zeros((LANES, GLA_HEADS * GLA_DK), F32).at[S_GLR:S_GLR + GLA_GATE_RANK].set(gla_w2)
    return w_big, w_small, gbias, w2pad, gla_b[None, :]


def _odd_weights(w_in):
    cuts = np.cumsum([ATTN_WIDTH, KV_WIDTH, KV_WIDTH, IDX_WIDTH, IDX_DH, IDX_HEADS])[:-1]
    q, k, v, qi, ki, wi = jnp.split(w_in, [int(c) for c in cuts], axis=1)
    w_big = jnp.concatenate([q, k, v, qi], axis=1).astype(BF16)
    w_small = _pad_cols(jnp.concatenate([ki, wi], axis=1), LANES).astype(BF16)
    return w_big, w_small


def _even_mixer(hb, w_in, conv_w, igate_b, fgate_b, gla_w2, gla_b, mng, gng):
    w_big, w_small, gbias, w2pad, glab = _even_weights(w_in, igate_b, fgate_b, gla_w2, gla_b)
    big = _matmul(hb, w_big, BF16, 1024, 1024)
    small = _matmul(hb, w_small, F32, 1024, LANES)
    return _mixer(big, small, conv_w, gbias, w2pad, glab, mng[None, :], gng[None, :])


def _odd_mixer(hb, pos, w_in):
    w_big, w_small = _odd_weights(w_in)
    big = _matmul(hb, w_big, BF16, 1024, 1024)
    small = _matmul(hb, w_small, F32, 1024, LANES)
    q, k, qi, kit, wi = _rope(big, small, pos)
    return _dsa(q, qi, wi, kit, k, big)


def kernel(x, positions, ab_w_in, ab_conv_w, ab_igate_b, ab_fgate_b, ab_gla_gate_w2, ab_gla_gate_b, ab_mlstm_norm_g, ab_gla_norm_g, ab_w_out, c_w_in, c_w_out, router_w, router_bias, moe_w_gate, moe_w_up, moe_w_down, ln_mix_g, ln_mix_b, ln_ffn_g, ln_ffn_b):
    bsz, t, d = x.shape
    outs = []
    rwt = router_w.T
    rbias = router_bias[:, None]
    flat = lambda w: w.reshape((DEPTH * N_EXPERTS,) + w.shape[2:])
    wg, wu, wd = _cast_expert_weights(flat(moe_w_gate), flat(moe_w_up), flat(moe_w_down))
    for bi in range(bsz):
        h = x[bi]
        hb = h.astype(BF16)
        pos = positions[bi][:, None]
        for layer in range(DEPTH):
            j = layer // 2
            if layer % 2 == 0:
                mix = _even_mixer(hb, ab_w_in[j], ab_conv_w[j], ab_igate_b[j], ab_fgate_b[j], ab_gla_gate_w2[j],
                                  ab_gla_gate_b[j], ab_mlstm_norm_g[j], ab_gla_norm_g[j])
                w_out = ab_w_out[j]
            else:
                mix = _odd_mixer(hb, pos, c_w_in[j])
                w_out = c_w_out[j]
            h, hb = _outproj_ln(mix, w_out.astype(BF16), h, ln_mix_g[layer], ln_mix_b[layer])
            grp, comb_t = _router(h, rwt, rbias)
            y = _moe(h, grp[0], comb_t.T, wg, wu, wd, layer * N_EXPERTS)
            h, hb = _residual_ln(h, y, ln_ffn_g[layer], ln_ffn_b[layer])
        outs.append(h)
    return jnp.stack(outs)
```

```python
import functools

import jax
import jax.numpy as jnp
import numpy as np
from jax import lax
from jax.experimental import pallas as pl
from jax.experimental.pallas import tpu as pltpu

F32 = jnp.float32
BF16 = jnp.bfloat16
HIGHEST = lax.Precision.HIGHEST

D_MODEL = 2048
DEPTH = 4
MLSTM_HEADS = 4
MLSTM_DK = 128
MLSTM_DV = 256
CONV_WIDTH = 4
GLA_HEADS = 4
GLA_DK = 128
GLA_DV = 256
GLA_GATE_RANK = 16
GLA_GATE_TAU = 16.0
CHUNK = 64
ATTN_HEADS = 16
ATTN_KV_HEADS = 4
ATTN_DH = 128
ROPE_DIM = ATTN_DH // 4
ROPE_THETA = 500000.0
IDX_HEADS = 16
IDX_DH = 64
IDX_ROPE_DIM = IDX_DH // 4
INDEX_TOPK = 256
N_EXPERTS = 16
N_GROUPS = 4
EXPERTS_PER_GROUP = N_EXPERTS // N_GROUPS
EXPERT_FF = 512
DEEPNORM_ALPHA = (2 * DEPTH) ** 0.25
LN_EPS = 1e-5
HEAD_NORM_EPS = 1e-6

LANES = 128
MIX_WIDTH = MLSTM_HEADS * MLSTM_DV + GLA_HEADS * GLA_DV
ATTN_WIDTH = ATTN_HEADS * ATTN_DH
KV_WIDTH = ATTN_KV_HEADS * ATTN_DH
IDX_WIDTH = IDX_HEADS * IDX_DH
Q_PER_KV = ATTN_HEADS // ATTN_KV_HEADS

E_MQ = 0
E_MK = E_MQ + MLSTM_HEADS * MLSTM_DK
E_MV = E_MK + MLSTM_HEADS * MLSTM_DK
E_MO = E_MV + MLSTM_HEADS * MLSTM_DV
E_GQ = E_MO + MLSTM_HEADS * MLSTM_DV
E_GK = E_GQ + GLA_HEADS * GLA_DK
E_GV = E_GK + GLA_HEADS * GLA_DK
E_GR = E_GV + GLA_HEADS * GLA_DV
E_BIG = E_GR + GLA_HEADS * GLA_DV
S_MI = 0
S_MF = S_MI + MLSTM_HEADS
S_GLR = S_MF + MLSTM_HEADS
O_Q = 0
O_K = O_Q + ATTN_WIDTH
O_V = O_K + KV_WIDTH
O_QI = O_V + KV_WIDTH
O_BIG = O_QI + IDX_WIDTH

VMEM_LIMIT = 56 * 1024 * 1024
DSA_VMEM_LIMIT = 60 * 1024 * 1024
SWEEP = 4
QK_SCALE_LOG2 = ATTN_DH ** -0.5 * 1.4426950408889634
INT_MIN = -(2**31)
NEG_BIG = -1e30


def _params(sem):
    return pltpu.CompilerParams(dimension_semantics=sem, vmem_limit_bytes=VMEM_LIMIT)


def _mm_kernel(x_ref, w_ref, o_ref):
    o_ref[...] = jnp.dot(x_ref[...], w_ref[...], preferred_element_type=F32).astype(o_ref.dtype)


def _matmul(x, w, out_dtype, tm, tn):
    t, k = x.shape
    n = w.shape[1]
    return pl.pallas_call(
        _mm_kernel,
        grid=(t // tm, n // tn),
        in_specs=[pl.BlockSpec((tm, k), lambda i, j: (i, 0)), pl.BlockSpec((k, tn), lambda i, j: (0, j))],
        out_specs=pl.BlockSpec((tm, tn), lambda i, j: (i, j)),
        out_shape=jax.ShapeDtypeStruct((t, n), out_dtype),
        compiler_params=_params(("parallel", "parallel")),
    )(x, w)


def _layer_norm(z, g, b):
    mu = jnp.mean(z, axis=-1, keepdims=True)
    zc = z - mu
    var = jnp.mean(zc * zc, axis=-1, keepdims=True)
    return zc * lax.rsqrt(var + LN_EPS) * g + b


def _outproj_ln_kernel(mix_ref, w_ref, h_ref, g_ref, b_ref, of_ref, ob_ref):
    y = jnp.dot(mix_ref[...], w_ref[...], preferred_element_type=F32)
    out = _layer_norm(DEEPNORM_ALPHA * h_ref[...] + y, g_ref[...], b_ref[...])
    of_ref[...] = out
    ob_ref[...] = out.astype(BF16)


def _outproj_ln(mix, w, h, g, b, tm=256):
    t, k = mix.shape
    d = w.shape[1]
    row = lambda i: (i, 0)
    fixed = lambda i: (0, 0)
    return pl.pallas_call(
        _outproj_ln_kernel,
        grid=(t // tm,),
        in_specs=[pl.BlockSpec((tm, k), row), pl.BlockSpec((k, d), fixed), pl.BlockSpec((tm, d), row),
                  pl.BlockSpec((1, d), fixed), pl.BlockSpec((1, d), fixed)],
        out_specs=[pl.BlockSpec((tm, d), row), pl.BlockSpec((tm, d), row)],
        out_shape=[jax.ShapeDtypeStruct((t, d), F32), jax.ShapeDtypeStruct((t, d), BF16)],
        compiler_params=_params(("parallel",)),
    )(mix, w, h, g.reshape(1, d), b.reshape(1, d))


def _router_kernel(h_ref, rwt_ref, bias_ref, grp_ref, comb_ref):
    logits = lax.dot_general(rwt_ref[...], h_ref[...], (((1,), (1,)), ((), ())),
                             precision=HIGHEST, preferred_element_type=F32)
    aff = jax.nn.sigmoid(logits)
    sel = aff + bias_ref[...]
    srow = [sel[e:e + 1, :] for e in range(N_EXPERTS)]
    arow = [aff[e:e + 1, :] for e in range(N_EXPERTS)]

    def top2_sum(v):
        best = v[0] + v[1]
        for a in range(len(v)):
            for b in range(a + 1, len(v)):
                if (a, b) != (0, 1):
                    best = jnp.maximum(best, v[a] + v[b])
        return best

    gscore = [top2_sum(srow[g * EXPERTS_PER_GROUP:(g + 1) * EXPERTS_PER_GROUP]) for g in range(N_GROUPS)]
    best = gscore[0]
    gi = jnp.zeros_like(best, dtype=jnp.int32)
    for g in range(1, N_GROUPS):
        upd = gscore[g] > best
        gi = jnp.where(upd, g, gi)
        best = jnp.where(upd, gscore[g], best)

    def pick(rows, k):
        out = rows[k]
        for g in range(1, N_GROUPS):
            out = jnp.where(gi == g, rows[g * EXPERTS_PER_GROUP + k], out)
        return out

    v = [pick(srow, k) for k in range(EXPERTS_PER_GROUP)]
    a = [pick(arow, k) for k in range(EXPERTS_PER_GROUP)]

    def first_argmax(vals):
        m = vals[0]
        for x in vals[1:]:
            m = jnp.maximum(m, x)
        loc = jnp.full_like(gi, len(vals) - 1)
        for k in range(len(vals) - 2, -1, -1):
            loc = jnp.where(vals[k] == m, k, loc)
        return loc

    loc1 = first_argmax(v)
    v2 = [jnp.where(loc1 == k, -jnp.inf, v[k]) for k in range(EXPERTS_PER_GROUP)]
    loc2 = first_argmax(v2)

    def take(vals, loc):
        out = vals[0]
        for k in range(1, len(vals)):
            out = jnp.where(loc == k, vals[k], out)
        return out

    g1 = take(a, loc1)
    g2 = take(a, loc2)
    tot = g1 + g2
    g1 = g1 / tot
    g2 = g2 / tot
    grp_ref[...] = gi
    krow = lax.broadcasted_iota(jnp.int32, comb_ref.shape, 0)
    comb_ref[...] = jnp.where(krow == loc1, g1, 0.0) + jnp.where(krow == loc2, g2, 0.0)


def _router(h, rwt, bias, tm=512):
    t, d = h.shape
    return pl.pallas_call(
        _router_kernel,
        grid=(t // tm,),
        in_specs=[pl.BlockSpec((tm, d), lambda i: (i, 0)), pl.BlockSpec((N_EXPERTS, d), lambda i: (0, 0)),
                  pl.BlockSpec((N_EXPERTS, 1), lambda i: (0, 0))],
        out_specs=[pl.BlockSpec((1, tm), lambda i: (0, i)), pl.BlockSpec((EXPERTS_PER_GROUP, tm), lambda i: (0, i))],
        out_shape=[jax.ShapeDtypeStruct((1, t), jnp.int32), jax.ShapeDtypeStruct((EXPERTS_PER_GROUP, t), F32)],
        compiler_params=_params(("parallel",)),
        name="router",
    )(h, rwt, bias)


def _cast_kernel(a_ref, b_ref, c_ref, ao_ref, bo_ref, co_ref):
    ao_ref[...] = a_ref[...].astype(BF16)
    bo_ref[...] = b_ref[...].astype(BF16)
    co_ref[...] = c_ref[...].astype(BF16)


def _cast_expert_weights(wg, wu, wd):
    n = wg.shape[0]
    spec = lambda a: pl.BlockSpec((1,) + a.shape[1:], lambda i: (i, 0, 0))
    return pl.pallas_call(
        _cast_kernel,
        grid=(n,),
        in_specs=[spec(wg), spec(wu), spec(wd)],
        out_specs=[spec(wg), spec(wu), spec(wd)],
        out_shape=[jax.ShapeDtypeStruct(a.shape, BF16) for a in (wg, wu, wd)],
        compiler_params=_params(("parallel",)),
        name="cast_expert_weights",
    )(wg, wu, wd)


def _moe_kernel(tg_ref, src_ref, nv_ref, nu_ref, h_hbm, comb_ref, wg_ref, wu_ref, wd_ref, y_hbm,
                xbuf, ybuf, gsem, ssem, *, tm):
    i = pl.program_id(0)
    e = pl.program_id(1)
    n_used = nu_ref[0]
    slot = lax.rem(i, 2)
    part = tm // EXPERTS_PER_GROUP
    cols = pl.ds(0, y_hbm.shape[1])

    def row_copy(r, tile):
        return pltpu.make_async_copy(ybuf.at[pl.ds(r, 1), cols], y_hbm.at[pl.ds(src_ref[tile * tm + r], 1), :],
                                     ssem.at[0])

    def gather_rows(tile, dst_slot, lo, hi):
        def body(r, carry):
            tok = src_ref[tile * tm + r]
            pltpu.make_async_copy(h_hbm.at[pl.ds(tok, 1), :], xbuf.at[dst_slot, pl.ds(r, 1), cols],
                                  gsem.at[dst_slot]).start()
            return carry

        lax.fori_loop(lo, hi, body, 0)

    def wait_gather(s):
        pltpu.make_async_copy(xbuf.at[s, :, cols], xbuf.at[s, :, cols], gsem.at[s]).wait()

    def wait_scatter(tile):
        def body(r, carry):
            row_copy(r, tile).wait()
            return carry

        lax.fori_loop(0, nv_ref[tile], body, 0)

    @pl.when(i < n_used)
    def _():
        @pl.when(jnp.logical_and(i == 0, e == 0))
        def _():
            gather_rows(0, 0, 0, tm)

        @pl.when(e == 0)
        def _():
            wait_gather(slot)

        nxt = jnp.minimum(i + 1, pl.num_programs(0) - 1)
        for r in range(part):
            row = e * part + r
            pltpu.make_async_copy(h_hbm.at[pl.ds(src_ref[nxt * tm + row], 1), :],
                                  xbuf.at[1 - slot, pl.ds(row, 1), cols], gsem.at[1 - slot]).start()

        comb = comb_ref[...]
        lane = lax.broadcasted_iota(jnp.int32, comb.shape, 1)
        c = jnp.sum(jnp.where(lane == e, comb, 0.0), axis=1, keepdims=True)
        x = xbuf[slot, :, cols].astype(BF16)
        hg = jnp.dot(x, wg_ref[0], preferred_element_type=F32)
        hu = jnp.dot(x, wu_ref[0], preferred_element_type=F32)
        hh = (hg * jax.nn.sigmoid(hg) * hu) * c
        contrib = jnp.dot(hh.astype(BF16), wd_ref[0], preferred_element_type=F32)

        @pl.when(e == 0)
        def _():
            @pl.when(i > 0)
            def _():
                wait_scatter(i - 1)

            ybuf[:, cols] = contrib

        @pl.when(e > 0)
        def _():
            ybuf[:, cols] += contrib

        @pl.when(e == EXPERTS_PER_GROUP - 1)
        def _():
            def body(r, carry):
                row_copy(r, i).start()
                return carry

            lax.fori_loop(0, nv_ref[i], body, 0)

            @pl.when(i == n_used - 1)
            def _():
                wait_scatter(i)
                wait_gather(1 - slot)


def _moe(h, grp, comb, wg, wu, wd, first_expert, tm=512):
    t, d = h.shape
    f = wg.shape[-1]
    nt = t // tm + N_GROUPS
    p = nt * tm
    onehot = (grp[:, None] == jnp.arange(N_GROUPS)[None, :]).astype(jnp.int32)
    csum = jnp.cumsum(onehot, axis=0)
    counts = csum[-1]
    rank = jnp.sum(csum * onehot, axis=1) - 1
    padded = (counts + tm - 1) // tm * tm
    ends = jnp.cumsum(padded)
    starts = ends - padded
    pos = jnp.sum(starts[None, :] * onehot, axis=1) + rank
    payload = jnp.concatenate([jnp.arange(t, dtype=jnp.int32)[:, None], lax.bitcast_convert_type(comb, jnp.int32)],
                              axis=1)
    table = jnp.zeros((p, 1 + EXPERTS_PER_GROUP), jnp.int32).at[pos].set(payload)
    src = table[:, 0]
    comb_s = lax.bitcast_convert_type(table[:, 1:], F32)
    tile_row0 = jnp.arange(nt, dtype=jnp.int32) * tm
    tile_group = jnp.minimum(jnp.sum((tile_row0[:, None] >= ends[None, :]).astype(jnp.int32), axis=1), N_GROUPS - 1)
    n_valid = jnp.clip(starts[tile_group] + counts[tile_group] - tile_row0, 0, tm).astype(jnp.int32)
    n_used = (ends[-1] // tm).astype(jnp.int32).reshape(1)

    wmap = lambda i, e, tg, s, nv, nu: (first_expert + tg[i] * EXPERTS_PER_GROUP + e, 0, 0)
    grid_spec = pltpu.PrefetchScalarGridSpec(
        num_scalar_prefetch=4,
        grid=(nt, EXPERTS_PER_GROUP),
        in_specs=[pl.BlockSpec(memory_space=pl.ANY),
                  pl.BlockSpec((tm, EXPERTS_PER_GROUP), lambda i, e, tg, s, nv, nu: (i, 0)),
                  pl.BlockSpec((1, d, f), wmap), pl.BlockSpec((1, d, f), wmap), pl.BlockSpec((1, f, d), wmap)],
        out_specs=pl.BlockSpec(memory_space=pl.ANY),
        scratch_shapes=[pltpu.VMEM((2, tm, d + LANES), F32), pltpu.VMEM((tm, d + LANES), F32),
                        pltpu.SemaphoreType.DMA((2,)), pltpu.SemaphoreType.DMA((1,))],
    )
    return pl.pallas_call(
        functools.partial(_moe_kernel, tm=tm),
        grid_spec=grid_spec,
        out_shape=jax.ShapeDtypeStruct((t, d), F32),
        compiler_params=_params(("arbitrary", "arbitrary")),
        name="moe_grouped",
    )(tile_group, src, n_valid, n_used, h, comb_s, wg, wu, wd)


def _residual_ln_kernel(h_ref, y_ref, g_ref, b_ref, of_ref, ob_ref):
    out = _layer_norm(DEEPNORM_ALPHA * h_ref[...] + y_ref[...], g_ref[...], b_ref[...])
    of_ref[...] = out
    ob_ref[...] = out.astype(BF16)


def _residual_ln(h, y, g, b, tm=512):
    t, d = h.shape
    row = lambda i: (i, 0)
    fixed = lambda i: (0, 0)
    return pl.pallas_call(
        _residual_ln_kernel,
        grid=(t // tm,),
        in_specs=[pl.BlockSpec((tm, d), row), pl.BlockSpec((tm, d), row), pl.BlockSpec((1, d), fixed),
                  pl.BlockSpec((1, d), fixed)],
        out_specs=[pl.BlockSpec((tm, d), row), pl.BlockSpec((tm, d), row)],
        out_shape=[jax.ShapeDtypeStruct((t, d), F32), jax.ShapeDtypeStruct((t, d), BF16)],
        compiler_params=_params(("parallel",)),
        name="residual_ln",
    )(h, y, g.reshape(1, d), b.reshape(1, d))


def _log_sigmoid(x):
    return jnp.minimum(x, 0.0) - jnp.log(1.0 + jnp.exp(-jnp.abs(x)))


def _dot_nt(a, b):
    return lax.dot_general(a, b, (((1,), (1,)), ((), ())), preferred_element_type=F32)


def _dot_tn(a, b, precision=None):
    return lax.dot_general(a, b, (((0,), (0,)), ((), ())), precision=precision, preferred_element_type=F32)


def _head_rms(x, gain):
    return x * lax.rsqrt(jnp.mean(x * x, axis=1, keepdims=True) + HEAD_NORM_EPS) * gain


def _mixer_kernel(big_ref, small_ref, convw_ref, gbias_ref, w2_ref, glab_ref, mng_ref, gng_ref, o_ref,
                  ext_s, qk_s, c_s, n_s, m_s, s_s, *, tb):
    i = pl.program_id(0)
    nqk = 2 * MLSTM_HEADS * MLSTM_DK
    tail = CONV_WIDTH - 1

    @pl.when(i == 0)
    def _():
        ext_s[0:8, :] = jnp.zeros((8, nqk), F32)
        c_s[...] = jnp.zeros_like(c_s)
        n_s[...] = jnp.zeros_like(n_s)
        m_s[...] = jnp.zeros_like(m_s)
        s_s[...] = jnp.zeros_like(s_s)

    ext_s[8:8 + tb, :] = big_ref[:, E_MQ:E_MQ + nqk].astype(F32)
    conv = ext_s[8:8 + tb, :] * convw_ref[tail:tail + 1, :]
    for j in range(tail):
        conv = conv + ext_s[8 - tail + j:8 - tail + j + tb, :] * convw_ref[j:j + 1, :]
    qk_s[...] = conv * jax.nn.sigmoid(conv)
    ext_s[0:8, :] = ext_s[tb:tb + 8, :]

    rows = lax.broadcasted_iota(jnp.int32, (CHUNK, CHUNK), 0)
    cols = lax.broadcasted_iota(jnp.int32, (CHUNK, CHUNK), 1)
    causal = rows >= cols
    ltri = causal.astype(F32)
    eye = (rows == cols).astype(F32)
    lane = lax.broadcasted_iota(jnp.int32, (CHUNK, LANES), 1)
    is_f = (lane >= S_MF) & (lane < S_MF + MLSTM_HEADS)

    def chunk(c, carry):
        r0 = pl.multiple_of(c * CHUNK, CHUNK)
        rs = pl.ds(r0, CHUNK)
        gates = small_ref[rs, :]
        pre = gates + gbias_ref[...]
        xg = jnp.where(is_f, _log_sigmoid(pre), pre)
        cum = jnp.dot(ltri, xg, precision=HIGHEST, preferred_element_type=F32)
        y = jnp.where(is_f, cum, xg)
        yt = _dot_tn(y, eye, precision=HIGHEST)

        for h in range(MLSTM_HEADS):
            q = qk_s[rs, h * MLSTM_DK:(h + 1) * MLSTM_DK]
            k = qk_s[rs, (MLSTM_HEADS + h) * MLSTM_DK:(MLSTM_HEADS + h + 1) * MLSTM_DK] * MLSTM_DK ** -0.5
            v = big_ref[rs, E_MV + h * MLSTM_DV:E_MV + (h + 1) * MLSTM_DV]
            ig_c = y[:, S_MI + h:S_MI + h + 1]
            b_c = y[:, S_MF + h:S_MF + h + 1]
            ig_r = yt[S_MI + h:S_MI + h + 1, :]
            b_r = yt[S_MF + h:S_MF + h + 1, :]
            b_last = y[CHUNK - 1:CHUNK, S_MF + h:S_MF + h + 1]
            c_prev = c_s[h]
            n_prev = n_s[h:h + 1, :]
            m_prev = m_s[h:h + 1, 0:1]
            d_log = jnp.where(causal, b_c - b_r + ig_r, -jnp.inf)
            inter = b_c + m_prev
            m_q = jnp.maximum(inter, jnp.max(d_log, axis=1, keepdims=True))
            qb = q.astype(BF16)
            w = _dot_nt(qb, k.astype(BF16)) * jnp.exp(d_log - m_q)
            s_int = jnp.exp(inter - m_q)
            num = jnp.dot(w.astype(BF16), v, preferred_element_type=F32) + s_int * jnp.dot(
                qb, c_prev.astype(BF16), preferred_element_type=F32)
            den = jnp.sum(w, axis=1, keepdims=True) + s_int * jnp.sum(q * n_prev, axis=1, keepdims=True)
            out = num / jnp.maximum(jnp.abs(den), jnp.exp(-m_q))
            a_c = b_last - b_c + ig_c
            m_loc = jnp.max(a_c, axis=0, keepdims=True)
            kw = k * jnp.exp(a_c - m_loc)
            c_loc = _dot_tn(kw.astype(BF16), v)
            n_loc = jnp.sum(kw, axis=0, keepdims=True)
            m_new = jnp.maximum(b_last + m_prev, m_loc)
            s_old = jnp.exp(b_last + m_prev - m_new)
            s_loc = jnp.exp(m_loc - m_new)
            c_s[h] = s_old * c_prev + s_loc * c_loc
            n_s[h:h + 1, :] = s_old * n_prev + s_loc * n_loc
            m_s[h:h + 1, :] = jnp.broadcast_to(m_new, (1, LANES))
            hn = _head_rms(out, mng_ref[:, h * MLSTM_DV:(h + 1) * MLSTM_DV])
            og = big_ref[rs, E_MO + h * MLSTM_DV:E_MO + (h + 1) * MLSTM_DV].astype(F32)
            o_ref[rs, h * MLSTM_DV:(h + 1) * MLSTM_DV] = (hn * jax.nn.sigmoid(og)).astype(BF16)

        z = jnp.dot(gates, w2_ref[...], precision=HIGHEST, preferred_element_type=F32) + glab_ref[...]
        log_a = _log_sigmoid(z) / GLA_GATE_TAU
        la_all = jnp.dot(ltri, log_a, precision=HIGHEST, preferred_element_type=F32)
        for h in range(GLA_HEADS):
            la = la_all[:, h * GLA_DK:(h + 1) * GLA_DK]
            la_last = la[CHUNK - 1:CHUNK, :]
            q = big_ref[rs, E_GQ + h * GLA_DK:E_GQ + (h + 1) * GLA_DK].astype(F32) * GLA_DK ** -0.5
            k = big_ref[rs, E_GK + h * GLA_DK:E_GK + (h + 1) * GLA_DK].astype(F32)
            v = big_ref[rs, E_GV + h * GLA_DV:E_GV + (h + 1) * GLA_DV]
            q_dec = (q * jnp.exp(la)).astype(BF16)
            k_dec = (k * jnp.exp(-la)).astype(BF16)
            k_end = (k * jnp.exp(la_last - la)).astype(BF16)
            attn = jnp.where(causal, _dot_nt(q_dec, k_dec), 0.0)
            st_prev = s_s[h]
            out = jnp.dot(attn.astype(BF16), v, preferred_element_type=F32) + _dot_nt(q_dec, st_prev.astype(BF16))
            s_s[h] = jnp.exp(la_last) * st_prev + _dot_tn(v, k_end)
            hn = _head_rms(out, gng_ref[:, h * GLA_DV:(h + 1) * GLA_DV])
            gr = big_ref[rs, E_GR + h * GLA_DV:E_GR + (h + 1) * GLA_DV].astype(F32)
            col = MLSTM_HEADS * MLSTM_DV + h * GLA_DV
            o_ref[rs, col:col + GLA_DV] = (hn * (gr * jax.nn.sigmoid(gr))).astype(BF16)
        return carry

    lax.fori_loop(0, tb // CHUNK, chunk, 0)


def _mixer(big, small, convw, gbias, w2pad, glab, mng, gng, tb=512):
    t = big.shape[0]
    nqk = 2 * MLSTM_HEADS * MLSTM_DK
    fixed = lambda i: (0, 0)
    return pl.pallas_call(
        functools.partial(_mixer_kernel, tb=tb),
        grid=(t // tb,),
        in_specs=[pl.BlockSpec((tb, E_BIG), lambda i: (i, 0)), pl.BlockSpec((tb, LANES), lambda i: (i, 0)),
                  pl.BlockSpec((CONV_WIDTH, nqk), fixed), pl.BlockSpec((1, LANES), fixed),
                  pl.BlockSpec((LANES, GLA_HEADS * GLA_DK), fixed), pl.BlockSpec((1, GLA_HEADS * GLA_DK), fixed),
                  pl.BlockSpec((1, MLSTM_HEADS * MLSTM_DV), fixed), pl.BlockSpec((1, GLA_HEADS * GLA_DV), fixed)],
        out_specs=pl.BlockSpec((tb, MIX_WIDTH), lambda i: (i, 0)),
        out_shape=jax.ShapeDtypeStruct((t, MIX_WIDTH), BF16),
        scratch_shapes=[pltpu.VMEM((tb + 8, nqk), F32), pltpu.VMEM((tb, nqk), F32),
                        pltpu.VMEM((MLSTM_HEADS, MLSTM_DK, MLSTM_DV), F32), pltpu.VMEM((8, MLSTM_DK), F32),
                        pltpu.VMEM((8, LANES), F32), pltpu.VMEM((GLA_HEADS, GLA_DV, GLA_DK), F32)],
        compiler_params=_params(("arbitrary",)),
    )(big, small, convw, gbias, w2pad, glab, mng, gng)


def _rope_kernel(big_ref, small_ref, pos_ref, freq_ref, sign_ref, q_ref, k_ref, qi_ref, kit_ref, wi_ref):
    pos = pos_ref[...].astype(F32)
    lane = lax.broadcasted_iota(jnp.int32, (pos.shape[0], LANES), 1)

    def tables(row):
        ang = pos * freq_ref[row:row + 1, :]
        return jnp.cos(ang), jnp.sin(ang) * sign_ref[row:row + 1, :]

    def rotate(x, cos, sin, half, period):
        up = pltpu.roll(x, LANES - half, axis=1)
        dn = pltpu.roll(x, half, axis=1)
        return x * cos + jnp.where((lane % period) < half, up, dn) * sin

    cos_a, sin_a = tables(0)
    for h in range(ATTN_HEADS):
        x = big_ref[:, O_Q + h * ATTN_DH:O_Q + (h + 1) * ATTN_DH].astype(F32)
        r = rotate(x, cos_a, sin_a, ROPE_DIM // 2, ATTN_DH) * QK_SCALE_LOG2
        q_ref[:, h * ATTN_DH:(h + 1) * ATTN_DH] = r.astype(BF16)
    for h in range(ATTN_KV_HEADS):
        x = big_ref[:, O_K + h * ATTN_DH:O_K + (h + 1) * ATTN_DH].astype(F32)
        k_ref[:, h * ATTN_DH:(h + 1) * ATTN_DH] = rotate(x, cos_a, sin_a, ROPE_DIM // 2, ATTN_DH).astype(BF16)
    cos_i, sin_i = tables(1)
    for h in range(IDX_WIDTH // LANES):
        x = big_ref[:, O_QI + h * LANES:O_QI + (h + 1) * LANES].astype(F32)
        qi_ref[:, h * LANES:(h + 1) * LANES] = rotate(x, cos_i, sin_i, IDX_ROPE_DIM // 2, IDX_DH).astype(BF16)
    cos_k, sin_k = tables(2)
    sm = small_ref[...]
    ki = rotate(sm, cos_k, sin_k, IDX_ROPE_DIM // 2, IDX_DH)[:, :IDX_DH].astype(BF16)
    eye = (lax.broadcasted_iota(jnp.int32, (IDX_DH, IDX_DH), 0)
           == lax.broadcasted_iota(jnp.int32, (IDX_DH, IDX_DH), 1)).astype(BF16)
    kit_ref[...] = _dot_nt(eye, ki).astype(BF16)
    wi_ref[...] = pltpu.roll(sm, LANES - IDX_DH, axis=1) * (IDX_HEADS ** -0.5 * IDX_DH ** -0.5)


def _rope_tables():
    def pattern(rot_dim, period, width):
        half = rot_dim // 2
        inv = 1.0 / (ROPE_THETA ** (jnp.arange(half, dtype=F32) / half))
        l = np.arange(LANES)
        inside = ((l % period) < rot_dim) & (l < width)
        freq = jnp.where(inside, inv[(l % period) % half], 0.0)
        sign = np.where(inside, np.where((l % period) < half, -1.0, 1.0), 0.0)
        return freq, jnp.asarray(sign, F32)

    fa, sa = pattern(ROPE_DIM, ATTN_DH, LANES)
    fi, si = pattern(IDX_ROPE_DIM, IDX_DH, LANES)
    fk, sk = pattern(IDX_ROPE_DIM, IDX_DH, IDX_DH)
    z = jnp.zeros((5, LANES), F32)
    return jnp.concatenate([jnp.stack([fa, fi, fk]), z]), jnp.concatenate([jnp.stack([sa, si, sk]), z])


def _rope(big, small, pos, tm=512):
    t = big.shape[0]
    freq, sign = _rope_tables()
    row = lambda i: (i, 0)
    fixed = lambda i: (0, 0)
    return pl.pallas_call(
        _rope_kernel,
        grid=(t // tm,),
        in_specs=[pl.BlockSpec((tm, O_BIG), row), pl.BlockSpec((tm, LANES), row), pl.BlockSpec((tm, 1), row),
                  pl.BlockSpec((8, LANES), fixed), pl.BlockSpec((8, LANES), fixed)],
        out_specs=[pl.BlockSpec((tm, ATTN_WIDTH), row), pl.BlockSpec((tm, KV_WIDTH), row),
                   pl.BlockSpec((tm, IDX_WIDTH), row), pl.BlockSpec((IDX_DH, tm), lambda i: (0, i)),
                   pl.BlockSpec((tm, LANES), row)],
        out_shape=[jax.ShapeDtypeStruct((t, ATTN_WIDTH), BF16), jax.ShapeDtypeStruct((t, KV_WIDTH), BF16),
                   jax.ShapeDtypeStruct((t, IDX_WIDTH), BF16), jax.ShapeDtypeStruct((IDX_DH, t), BF16),
                   jax.ShapeDtypeStruct((t, LANES), F32)],
        compiler_params=_params(("parallel",)),
        name="rope",
    )(big, small, pos, freq, sign)


def _dsa_kernel(q_ref, qi_ref, wi_ref, kit_ref, k_ref, v_ref, o_ref,
                keys_s, hi_s, cut_s, bias_s, p_s, alpha_s, m_s, acc_s, *, tq, tk, top_k, n_key_slots):
    qb = pl.program_id(0)
    n_tiles = (qb * tq + tq - 1) // tk + 1

    w = wi_ref[...]
    qpos = qb * tq + lax.broadcasted_iota(jnp.int32, (tq, tk), 0)
    kiota = lax.broadcasted_iota(jnp.int32, (tq, tk), 1)

    def score_tile(c, carry):
        c0 = pl.multiple_of(c * tk, tk)
        kt = kit_ref[:, pl.ds(c0, tk)]
        sc = jnp.zeros((tq, tk), F32)
        for h in range(IDX_HEADS):
            lg = jnp.dot(qi_ref[:, h * IDX_DH:(h + 1) * IDX_DH], kt, preferred_element_type=F32)
            sc = sc + jnp.maximum(lg, 0.0) * w[:, h:h + 1]
        bits = pltpu.bitcast(sc, jnp.int32)
        key = bits ^ ((bits >> 31) & 0x7FFFFFFF)
        key = jnp.where(kiota + c0 <= qpos, key, INT_MIN)
        keys_s[:, pl.ds(c0, tk)] = key
        hi_s[:, pl.ds(c0, tk)] = (key >> 16).astype(jnp.int16)
        return carry

    lax.fori_loop(0, n_tiles, score_tile, 0)

    n_sweeps = (n_tiles + SWEEP - 1) // SWEEP

    def pad_tile(c, carry):
        c0 = pl.multiple_of(c * tk, tk)
        keys_s[:, pl.ds(c0, tk)] = jnp.full((tq, tk), INT_MIN, jnp.int32)
        hi_s[:, pl.ds(c0, tk)] = jnp.full((tq, tk), -(2 ** 15), jnp.int16)
        return carry

    lax.fori_loop(n_tiles, n_sweeps * SWEEP, pad_tile, 0)

    def lane_total(acc):
        ones_sq = jnp.ones((LANES, LANES), BF16)
        return jnp.dot(acc, ones_sq, preferred_element_type=F32).astype(jnp.int32)

    def count_ge(cand):
        def body(c, acc):
            c0 = pl.multiple_of(c * (SWEEP * tk), SWEEP * tk)
            for u in range(SWEEP * tk // LANES):
                ge = keys_s[:, pl.ds(c0 + u * LANES, LANES)] >= cand
                acc = acc + ge.astype(jnp.int32)
            return acc

        acc = lax.fori_loop(0, n_sweeps, body, jnp.zeros((tq, LANES), jnp.int32))
        return lane_total(acc.astype(F32).astype(BF16))

    def count_hi_ge(cand):
        c16 = cand.astype(jnp.int16)
        one = jnp.ones((tq, LANES), BF16)
        nil = jnp.zeros((tq, LANES), BF16)

        def body(c, acc):
            c0 = pl.multiple_of(c * (SWEEP * tk), SWEEP * tk)
            for u in range(SWEEP * tk // LANES):
                acc = acc + jnp.where(hi_s[:, pl.ds(c0 + u * LANES, LANES)] >= c16, one, nil)
            return acc

        acc = lax.fori_loop(0, n_sweeps, body, nil)
        return lane_total(acc)

    def bit_step(count_fn, shift, b, ubase, kept):
        ucand = ubase | jnp.left_shift(jnp.int32(1), 31 - b)
        cnt = count_fn((ucand ^ INT_MIN) >> shift)
        ok = cnt >= top_k
        return jnp.where(ok, ucand, ubase), jnp.where(ok, cnt, kept)

    ubase = jnp.zeros((tq, LANES), jnp.int32)
    kept = jnp.full((tq, LANES), n_sweeps * SWEEP * tk, jnp.int32)
    ubase, kept = lax.fori_loop(0, 16, lambda b, st: bit_step(count_hi_ge, 16, b, *st), (ubase, kept))

    def unsettled(state):
        b, _, kept = state
        return jnp.logical_and(b < 32, jnp.max(jnp.abs(kept - top_k)) > 0)

    def four_bits(state):
        b, ubase, kept = state
        for j in range(4):
            ubase, kept = bit_step(count_ge, 0, b + j, ubase, kept)
        return b + 4, ubase, kept

    _, ubase, kept = lax.while_loop(unsettled, four_bits, (jnp.int32(16), ubase, kept))
    thr = ubase ^ INT_MIN
    thr = jnp.maximum(thr, INT_MIN + 1)
    thr_t = jnp.tile(thr, (1, tk // LANES))

    cut_s[...] = jnp.full(cut_s.shape, 2 ** 31 - 1, jnp.int32)
    lane_pos = lax.broadcasted_iota(jnp.int32, (tq, LANES), 1)

    @pl.when(jnp.max(jnp.abs(kept - top_k)) > 0)
    def _():
        need = top_k - count_ge(thr + 1)

        def count_tied_before(pos):
            def body(c, acc):
                c0 = pl.multiple_of(c * (SWEEP * tk), SWEEP * tk)
                for u in range(SWEEP * tk // LANES):
                    tied = keys_s[:, pl.ds(c0 + u * LANES, LANES)] == thr
                    acc = acc + jnp.where(tied, (lane_pos + (c0 + u * LANES) < pos).astype(jnp.int32), 0)
                return acc

            acc = lax.fori_loop(0, n_sweeps, body, jnp.zeros((tq, LANES), jnp.int32))
            return lane_total(acc.astype(F32).astype(BF16))

        pos_bits = max(1, (n_key_slots - 1).bit_length())

        def pos_step(b, cut):
            cand = cut | jnp.left_shift(jnp.int32(1), pos_bits - 1 - b)
            return jnp.where(count_tied_before(cand) < need, cand, cut)

        cut_s[...] = lax.fori_loop(0, pos_bits, pos_step, jnp.zeros((tq, LANES), jnp.int32))

    cut_t = jnp.tile(cut_s[...], (1, tk // LANES))

    m_s[...] = jnp.full_like(m_s, NEG_BIG)
    acc_s[...] = jnp.zeros_like(acc_s)
    ones = jnp.ones((tk, ATTN_DH), BF16)

    def flash_tile(c, carry):
        c0 = pl.multiple_of(c * tk, tk)
        key = keys_s[:, pl.ds(c0, tk)]
        keep = (key > thr_t) | ((key == thr_t) & (kiota + c0 <= cut_t))
        bias_s[...] = jnp.where(keep, 0.0, NEG_BIG)
        for g in range(ATTN_KV_HEADS):
            qg = jnp.concatenate(
                [q_ref[:, (g * Q_PER_KV + r) * ATTN_DH:(g * Q_PER_KV + r + 1) * ATTN_DH] for r in range(Q_PER_KV)],
                axis=0)
            s = _dot_nt(qg, k_ref[pl.ds(c0, tk), g * ATTN_DH:(g + 1) * ATTN_DH])
            for r in range(Q_PER_KV):
                rows = slice(r * tq, (r + 1) * tq)
                sr = s[rows, :] + bias_s[...]
                m_prev = m_s[g, rows, :]
                m_new = jnp.maximum(m_prev, jnp.max(sr, axis=1, keepdims=True))
                p_s[g, rows, :] = jnp.exp2(sr - jnp.tile(m_new, (1, tk // LANES))).astype(BF16)
                alpha_s[g, rows, :] = jnp.exp2(m_prev - m_new)
                m_s[g, rows, :] = m_new
            v_ext = jnp.concatenate([v_ref[pl.ds(c0, tk), g * ATTN_DH:(g + 1) * ATTN_DH], ones], axis=1)
            pv = jnp.dot(p_s[g], v_ext, preferred_element_type=F32)
            acc_s[g] = jnp.tile(alpha_s[g], (1, 2)) * acc_s[g] + pv
        return carry

    lax.fori_loop(0, n_tiles, flash_tile, 0)

    for g in range(ATTN_KV_HEADS):
        a = acc_s[g]
        res = a[:, :ATTN_DH] / a[:, ATTN_DH:]
        for r in range(Q_PER_KV):
            hcol = (g * Q_PER_KV + r) * ATTN_DH
            o_ref[:, hcol:hcol + ATTN_DH] = res[r * tq:(r + 1) * tq, :].astype(BF16)


def _dsa(q, qi, wi, kit, k, big, tq=128, tk=512):
    t = q.shape[0]
    tcap = -(-t // (SWEEP * tk)) * (SWEEP * tk)
    assert tcap // LANES <= 256, "per-lane key counts are accumulated in bf16"
    top_k = min(INDEX_TOPK, t // 4)
    v_blk = O_V // KV_WIDTH
    qrow = lambda i: (i, 0)
    once = pl.Buffered(1)
    rq = Q_PER_KV * tq
    return pl.pallas_call(
        functools.partial(_dsa_kernel, tq=tq, tk=tk, top_k=top_k, n_key_slots=tcap),
        grid=(t // tq,),
        in_specs=[pl.BlockSpec((tq, ATTN_WIDTH), qrow), pl.BlockSpec((tq, IDX_WIDTH), qrow),
                  pl.BlockSpec((tq, LANES), qrow),
                  pl.BlockSpec((IDX_DH, t), lambda i: (0, 0), pipeline_mode=once),
                  pl.BlockSpec((t, KV_WIDTH), lambda i: (0, 0), pipeline_mode=once),
                  pl.BlockSpec((t, KV_WIDTH), lambda i: (0, v_blk), pipeline_mode=once)],
        out_specs=pl.BlockSpec((tq, ATTN_WIDTH), qrow),
        out_shape=jax.ShapeDtypeStruct((t, ATTN_WIDTH), BF16),
        scratch_shapes=[pltpu.VMEM((tq, tcap + LANES), jnp.int32), pltpu.VMEM((tq, tcap + LANES), jnp.int16),
                        pltpu.VMEM((tq, LANES), jnp.int32),
                        pltpu.VMEM((tq, tk), F32), pltpu.VMEM((ATTN_KV_HEADS, rq, tk), BF16),
                        pltpu.VMEM((ATTN_KV_HEADS, rq, LANES), F32), pltpu.VMEM((ATTN_KV_HEADS, rq, LANES), F32),
                        pltpu.VMEM((ATTN_KV_HEADS, rq, 2 * ATTN_DH), F32)],
        compiler_params=pltpu.CompilerParams(dimension_semantics=("parallel",), vmem_limit_bytes=DSA_VMEM_LIMIT),
        name="dsa_attention",
    )(q, qi, wi, kit, k, big)


def _pad_cols(w, width):
    return jnp.pad(w, ((0, 0), (0, width - w.shape[1])))


def _even_weights(w_in, igate_b, fgate_b, gla_w2, gla_b):
    cuts = np.cumsum([512, 512, 1024, 4, 4, 1024, 512, 512, 1024, 16, 1024])[:-1]
    mq, mk, mv, mi, mf, mo, gq, gk, gv, glr, gr = jnp.split(w_in, [int(c) for c in cuts], axis=1)
    w_big = jnp.concatenate([mq, mk, mv, mo, gq, gk, gv, gr], axis=1).astype(BF16)
    w_small = _pad_cols(jnp.concatenate([mi, mf, glr], axis=1), LANES).astype(BF16)
    gbias = _pad_cols(jnp.concatenate([igate_b, fgate_b])[None, :], LANES)
    w2pad = jnp.zeros((LANES, GLA_HEADS * GLA_DK), F32).at[S_GLR:S_GLR + GLA_GATE_RANK].set(gla_w2)
    return w_big, w_small, gbias, w2pad, gla_b[None, :]


def _odd_weights(w_in):
    cuts = np.cumsum([ATTN_WIDTH, KV_WIDTH, KV_WIDTH, IDX_WIDTH, IDX_DH, IDX_HEADS])[:-1]
    q, k, v, qi, ki, wi = jnp.split(w_in, [int(c) for c in cuts], axis=1)
    w_big = jnp.concatenate([q, k, v, qi], axis=1).astype(BF16)
    w_small = _pad_cols(jnp.concatenate([ki, wi], axis=1), LANES).astype(BF16)
    return w_big, w_small


def _even_mixer(hb, w_in, conv_w, igate_b, fgate_b, gla_w2, gla_b, mng, gng):
    w_big, w_small, gbias, w2pad, glab = _even_weights(w_in, igate_b, fgate_b, gla_w2, gla_b)
    big = _matmul(hb, w_big, BF16, 1024, 1024)
    small = _matmul(hb, w_small, F32, 1024, LANES)
    return _mixer(big, small, conv_w, gbias, w2pad, glab, mng[None, :], gng[None, :])


def _odd_mixer(hb, pos, w_in):
    w_big, w_small = _odd_weights(w_in)
    big = _matmul(hb, w_big, BF16, 1024, 1024)
    small = _matmul(hb, w_small, F32, 1024, LANES)
    q, k, qi, kit, wi = _rope(big, small, pos)
    return _dsa(q, qi, wi, kit, k, big)


def kernel(x, positions, ab_w_in, ab_conv_w, ab_igate_b, ab_fgate_b, ab_gla_gate_w2, ab_gla_gate_b, ab_mlstm_norm_g, ab_gla_norm_g, ab_w_out, c_w_in, c_w_out, router_w, router_bias, moe_w_gate, moe_w_up, moe_w_down, ln_mix_g, ln_mix_b, ln_ffn_g, ln_ffn_b):
    bsz, t, d = x.shape
    outs = []
    rwt = router_w.T
    rbias = router_bias[:, None]
    flat = lambda w: w.reshape((DEPTH * N_EXPERTS,) + w.shape[2:])
    wg, wu, wd = _cast_expert_weights(flat(moe_w_gate), flat(moe_w_up), flat(moe_w_down))
    for bi in range(bsz):
        h = x[bi]
        hb = h.astype(BF16)
        pos = positions[bi][:, None]
        for layer in range(DEPTH):
            j = layer // 2
            if layer % 2 == 0:
                mix = _even_mixer(hb, ab_w_in[j], ab_conv_w[j], ab_igate_b[j], ab_fgate_b[j], ab_gla_gate_w2[j],
                                  ab_gla_gate_b[j], ab_mlstm_norm_g[j], ab_gla_norm_g[j])
                w_out = ab_w_out[j]
            else:
                mix = _odd_mixer(hb, pos, c_w_in[j])
                w_out = c_w_out[j]
            h, hb = _outproj_ln(mix, w_out.astype(BF16), h, ln_mix_g[layer], ln_mix_b[layer])
            grp, comb_t = _router(h, rwt, rbias)
            y = _moe(h, grp[0], comb_t.T, wg, wu, wd, layer * N_EXPERTS)
            h, hb = _residual_ln(h, y, ln_ffn_g[layer], ln_ffn_b[layer])
        outs.append(h)
    return jnp.stack(outs)
```

```python
import functools

import jax
import jax.numpy as jnp
import numpy as np
from jax import lax
from jax.experimental import pallas as pl
from jax.experimental.pallas import tpu as pltpu

F32 = jnp.float32
BF16 = jnp.bfloat16
HIGHEST = lax.Precision.HIGHEST

D_MODEL = 2048
DEPTH = 4
MLSTM_HEADS = 4
MLSTM_DK = 128
MLSTM_DV = 256
CONV_WIDTH = 4
GLA_HEADS = 4
GLA_DK = 128
GLA_DV = 256
GLA_GATE_RANK = 16
GLA_GATE_TAU = 16.0
CHUNK = 64
ATTN_HEADS = 16
ATTN_KV_HEADS = 4
ATTN_DH = 128
ROPE_DIM = ATTN_DH // 4
ROPE_THETA = 500000.0
IDX_HEADS = 16
IDX_DH = 64
IDX_ROPE_DIM = IDX_DH // 4
INDEX_TOPK = 256
N_EXPERTS = 16
N_GROUPS = 4
EXPERTS_PER_GROUP = N_EXPERTS // N_GROUPS
EXPERT_FF = 512
DEEPNORM_ALPHA = (2 * DEPTH) ** 0.25
LN_EPS = 1e-5
HEAD_NORM_EPS = 1e-6

LANES = 128
MIX_WIDTH = MLSTM_HEADS * MLSTM_DV + GLA_HEADS * GLA_DV
ATTN_WIDTH = ATTN_HEADS * ATTN_DH
KV_WIDTH = ATTN_KV_HEADS * ATTN_DH
IDX_WIDTH = IDX_HEADS * IDX_DH
Q_PER_KV = ATTN_HEADS // ATTN_KV_HEADS

E_MQ = 0
E_MK = E_MQ + MLSTM_HEADS * MLSTM_DK
E_MV = E_MK + MLSTM_HEADS * MLSTM_DK
E_MO = E_MV + MLSTM_HEADS * MLSTM_DV
E_GQ = E_MO + MLSTM_HEADS * MLSTM_DV
E_GK = E_GQ + GLA_HEADS * GLA_DK
E_GV = E_GK + GLA_HEADS * GLA_DK
E_GR = E_GV + GLA_HEADS * GLA_DV
E_BIG = E_GR + GLA_HEADS * GLA_DV
S_MI = 0
S_MF = S_MI + MLSTM_HEADS
S_GLR = S_MF + MLSTM_HEADS
O_Q = 0
O_K = O_Q + ATTN_WIDTH
O_V = O_K + KV_WIDTH
O_QI = O_V + KV_WIDTH
O_BIG = O_QI + IDX_WIDTH

VMEM_LIMIT = 56 * 1024 * 1024
DSA_VMEM_LIMIT = 60 * 1024 * 1024
SWEEP = 4
QK_SCALE_LOG2 = ATTN_DH ** -0.5 * 1.4426950408889634
INT_MIN = -(2**31)
NEG_BIG = -1e30


def _params(sem):
    return pltpu.CompilerParams(dimension_semantics=sem, vmem_limit_bytes=VMEM_LIMIT)


def _mm_kernel(x_ref, w_ref, o_ref):
    o_ref[...] = jnp.dot(x_ref[...], w_ref[...], preferred_element_type=F32).astype(o_ref.dtype)


def _matmul(x, w, out_dtype, tm, tn):
    t, k = x.shape
    n = w.shape[1]
    return pl.pallas_call(
        _mm_kernel,
        grid=(t // tm, n // tn),
        in_specs=[pl.BlockSpec((tm, k), lambda i, j: (i, 0)), pl.BlockSpec((k, tn), lambda i, j: (0, j))],
        out_specs=pl.BlockSpec((tm, tn), lambda i, j: (i, j)),
        out_shape=jax.ShapeDtypeStruct((t, n), out_dtype),
        compiler_params=_params(("parallel", "parallel")),
    )(x, w)


def _layer_norm(z, g, b):
    mu = jnp.mean(z, axis=-1, keepdims=True)
    zc = z - mu
    var = jnp.mean(zc * zc, axis=-1, keepdims=True)
    return zc * lax.rsqrt(var + LN_EPS) * g + b


def _outproj_ln_kernel(mix_ref, w_ref, h_ref, g_ref, b_ref, of_ref):
    y = jnp.dot(mix_ref[...], w_ref[...], preferred_element_type=F32)
    of_ref[...] = _layer_norm(DEEPNORM_ALPHA * h_ref[...] + y, g_ref[...], b_ref[...])


def _outproj_ln(mix, w, h, g, b, tm=256):
    t, k = mix.shape
    d = w.shape[1]
    row = lambda i: (i, 0)
    fixed = lambda i: (0, 0)
    return pl.pallas_call(
        _outproj_ln_kernel,
        grid=(t // tm,),
        in_specs=[pl.BlockSpec((tm, k), row), pl.BlockSpec((k, d), fixed), pl.BlockSpec((tm, d), row),
                  pl.BlockSpec((1, d), fixed), pl.BlockSpec((1, d), fixed)],
        out_specs=pl.BlockSpec((tm, d), row),
        out_shape=jax.ShapeDtypeStruct((t, d), F32),
        compiler_params=_params(("parallel",)),
        name="outproj_ln",
    )(mix, w, h, g.reshape(1, d), b.reshape(1, d))


def _router_kernel(h_ref, rwt_ref, bias_ref, grp_ref, comb_ref):
    logits = lax.dot_general(rwt_ref[...], h_ref[...], (((1,), (1,)), ((), ())),
                             precision=HIGHEST, preferred_element_type=F32)
    aff = jax.nn.sigmoid(logits)
    sel = aff + bias_ref[...]
    srow = [sel[e:e + 1, :] for e in range(N_EXPERTS)]
    arow = [aff[e:e + 1, :] for e in range(N_EXPERTS)]

    def top2_sum(v):
        best = v[0] + v[1]
        for a in range(len(v)):
            for b in range(a + 1, len(v)):
                if (a, b) != (0, 1):
                    best = jnp.maximum(best, v[a] + v[b])
        return best

    gscore = [top2_sum(srow[g * EXPERTS_PER_GROUP:(g + 1) * EXPERTS_PER_GROUP]) for g in range(N_GROUPS)]
    best = gscore[0]
    gi = jnp.zeros_like(best, dtype=jnp.int32)
    for g in range(1, N_GROUPS):
        upd = gscore[g] > best
        gi = jnp.where(upd, g, gi)
        best = jnp.where(upd, gscore[g], best)

    def pick(rows, k):
        out = rows[k]
        for g in range(1, N_GROUPS):
            out = jnp.where(gi == g, rows[g * EXPERTS_PER_GROUP + k], out)
        return out

    v = [pick(srow, k) for k in range(EXPERTS_PER_GROUP)]
    a = [pick(arow, k) for k in range(EXPERTS_PER_GROUP)]

    def first_argmax(vals):
        m = vals[0]
        for x in vals[1:]:
            m = jnp.maximum(m, x)
        loc = jnp.full_like(gi, len(vals) - 1)
        for k in range(len(vals) - 2, -1, -1):
            loc = jnp.where(vals[k] == m, k, loc)
        return loc

    loc1 = first_argmax(v)
    v2 = [jnp.where(loc1 == k, -jnp.inf, v[k]) for k in range(EXPERTS_PER_GROUP)]
    loc2 = first_argmax(v2)

    def take(vals, loc):
        out = vals[0]
        for k in range(1, len(vals)):
            out = jnp.where(loc == k, vals[k], out)
        return out

    g1 = take(a, loc1)
    g2 = take(a, loc2)
    tot = g1 + g2
    g1 = g1 / tot
    g2 = g2 / tot
    grp_ref[...] = gi
    krow = lax.broadcasted_iota(jnp.int32, comb_ref.shape, 0)
    comb_ref[...] = jnp.where(krow == loc1, g1, 0.0) + jnp.where(krow == loc2, g2, 0.0)


def _router(h, rwt, bias, tm=512):
    t, d = h.shape
    return pl.pallas_call(
        _router_kernel,
        grid=(t // tm,),
        in_specs=[pl.BlockSpec((tm, d), lambda i: (i, 0)), pl.BlockSpec((N_EXPERTS, d), lambda i: (0, 0)),
                  pl.BlockSpec((N_EXPERTS, 1), lambda i: (0, 0))],
        out_specs=[pl.BlockSpec((1, tm), lambda i: (0, i)), pl.BlockSpec((EXPERTS_PER_GROUP, tm), lambda i: (0, i))],
        out_shape=[jax.ShapeDtypeStruct((1, t), jnp.int32), jax.ShapeDtypeStruct((EXPERTS_PER_GROUP, t), F32)],
        compiler_params=_params(("parallel",)),
        name="router",
    )(h, rwt, bias)


def _cast_kernel(a_ref, b_ref, c_ref, ao_ref, bo_ref, co_ref):
    ao_ref[...] = a_ref[...].astype(BF16)
    bo_ref[...] = b_ref[...].astype(BF16)
    co_ref[...] = c_ref[...].astype(BF16)


def _cast_expert_weights(wg, wu, wd):
    n = wg.shape[0]
    spec = lambda a: pl.BlockSpec((1,) + a.shape[1:], lambda i: (i, 0, 0))
    return pl.pallas_call(
        _cast_kernel,
        grid=(n,),
        in_specs=[spec(wg), spec(wu), spec(wd)],
        out_specs=[spec(wg), spec(wu), spec(wd)],
        out_shape=[jax.ShapeDtypeStruct(a.shape, BF16) for a in (wg, wu, wd)],
        compiler_params=_params(("parallel",)),
        name="cast_expert_weights",
    )(wg, wu, wd)


def _moe_kernel(tg_ref, src_ref, nv_ref, nu_ref, h_hbm, comb_ref, wg_ref, wu_ref, wd_ref, y_hbm,
                xbuf, ybuf, gsem, ssem, *, tm):
    i = pl.program_id(0)
    e = pl.program_id(1)
    n_used = nu_ref[0]
    slot = lax.rem(i, 2)
    part = tm // EXPERTS_PER_GROUP
    cols = pl.ds(0, y_hbm.shape[1])

    def row_copy(r, tile):
        s = lax.rem(tile, 2)
        return pltpu.make_async_copy(ybuf.at[s, pl.ds(r, 1), cols], y_hbm.at[pl.ds(src_ref[tile * tm + r], 1), :],
                                     ssem.at[s])

    def gather_rows(tile, dst_slot, lo, hi):
        def body(r, carry):
            tok = src_ref[tile * tm + r]
            pltpu.make_async_copy(h_hbm.at[pl.ds(tok, 1), :], xbuf.at[dst_slot, pl.ds(r, 1), cols],
                                  gsem.at[dst_slot]).start()
            return carry

        lax.fori_loop(lo, hi, body, 0)

    def wait_gather(s):
        pltpu.make_async_copy(xbuf.at[s, :, cols], xbuf.at[s, :, cols], gsem.at[s]).wait()

    def wait_scatter(tile):
        s = lax.rem(tile, 2)
        nv = nv_ref[tile]
        size = tm
        while size >= 1:
            @pl.when((nv & size) != 0)
            def _():
                batch = ybuf.at[s, pl.ds(0, size), cols]
                pltpu.make_async_copy(batch, batch, ssem.at[s]).wait()

            size //= 2

    @pl.when(i < n_used)
    def _():
        @pl.when(jnp.logical_and(i == 0, e == 0))
        def _():
            gather_rows(0, 0, 0, tm)

        @pl.when(e == 0)
        def _():
            wait_gather(slot)

        nxt = jnp.minimum(i + 1, pl.num_programs(0) - 1)
        for r in range(part):
            row = e * part + r
            pltpu.make_async_copy(h_hbm.at[pl.ds(src_ref[nxt * tm + row], 1), :],
                                  xbuf.at[1 - slot, pl.ds(row, 1), cols], gsem.at[1 - slot]).start()

        comb = comb_ref[...]
        lane = lax.broadcasted_iota(jnp.int32, comb.shape, 1)
        c = jnp.sum(jnp.where(lane == e, comb, 0.0), axis=1, keepdims=True)
        x = xbuf[slot, :, cols].astype(BF16)
        hg = jnp.dot(x, wg_ref[0], preferred_element_type=F32)
        hu = jnp.dot(x, wu_ref[0], preferred_element_type=F32)
        hh = (hg * jax.nn.sigmoid(hg) * hu) * c
        contrib = jnp.dot(hh.astype(BF16), wd_ref[0], preferred_element_type=F32)

        @pl.when(e == 0)
        def _():
            @pl.when(i > 1)
            def _():
                wait_scatter(i - 2)

            ybuf[slot, :, cols] = contrib

        @pl.when(e > 0)
        def _():
            ybuf[slot, :, cols] += contrib

        @pl.when(e == EXPERTS_PER_GROUP - 1)
        def _():
            def body(r, carry):
                row_copy(r, i).start()
                return carry

            lax.fori_loop(0, nv_ref[i], body, 0)

            @pl.when(i == n_used - 1)
            def _():
                @pl.when(i > 0)
                def _():
                    wait_scatter(i - 1)

                wait_scatter(i)
                wait_gather(1 - slot)


def _moe(h, grp, comb, wg, wu, wd, first_expert, tm=512):
    t, d = h.shape
    f = wg.shape[-1]
    nt = t // tm + N_GROUPS
    p = nt * tm
    onehot = (grp[:, None] == jnp.arange(N_GROUPS)[None, :]).astype(jnp.int32)
    csum = jnp.cumsum(onehot, axis=0)
    counts = csum[-1]
    rank = jnp.sum(csum * onehot, axis=1) - 1
    padded = (counts + tm - 1) // tm * tm
    ends = jnp.cumsum(padded)
    starts = ends - padded
    pos = jnp.sum(starts[None, :] * onehot, axis=1) + rank
    payload = jnp.concatenate([jnp.arange(t, dtype=jnp.int32)[:, None], lax.bitcast_convert_type(comb, jnp.int32)],
                              axis=1)
    table = jnp.zeros((p, 1 + EXPERTS_PER_GROUP), jnp.int32).at[pos].set(payload)
    src = table[:, 0]
    comb_s = lax.bitcast_convert_type(table[:, 1:], F32)
    tile_row0 = jnp.arange(nt, dtype=jnp.int32) * tm
    tile_group = jnp.minimum(jnp.sum((tile_row0[:, None] >= ends[None, :]).astype(jnp.int32), axis=1), N_GROUPS - 1)
    n_valid = jnp.clip(starts[tile_group] + counts[tile_group] - tile_row0, 0, tm).astype(jnp.int32)
    n_used = (ends[-1] // tm).astype(jnp.int32).reshape(1)

    wmap = lambda i, e, tg, s, nv, nu: (first_expert + tg[i] * EXPERTS_PER_GROUP + e, 0, 0)
    grid_spec = pltpu.PrefetchScalarGridSpec(
        num_scalar_prefetch=4,
        grid=(nt, EXPERTS_PER_GROUP),
        in_specs=[pl.BlockSpec(memory_space=pl.ANY),
                  pl.BlockSpec((tm, EXPERTS_PER_GROUP), lambda i, e, tg, s, nv, nu: (i, 0)),
                  pl.BlockSpec((1, d, f), wmap), pl.BlockSpec((1, d, f), wmap), pl.BlockSpec((1, f, d), wmap)],
        out_specs=pl.BlockSpec(memory_space=pl.ANY),
        scratch_shapes=[pltpu.VMEM((2, tm, d + LANES), F32), pltpu.VMEM((2, tm, d + LANES), F32),
                        pltpu.SemaphoreType.DMA((2,)), pltpu.SemaphoreType.DMA((2,))],
    )
    return pl.pallas_call(
        functools.partial(_moe_kernel, tm=tm),
        grid_spec=grid_spec,
        out_shape=jax.ShapeDtypeStruct((t, d), F32),
        compiler_params=_params(("arbitrary", "arbitrary")),
        name="moe_grouped",
    )(tile_group, src, n_valid, n_used, h, comb_s, wg, wu, wd)


def _residual_ln_kernel(h_ref, y_ref, g_ref, b_ref, of_ref, ob_ref):
    out = _layer_norm(DEEPNORM_ALPHA * h_ref[...] + y_ref[...], g_ref[...], b_ref[...])
    of_ref[...] = out
    ob_ref[...] = out.astype(BF16)


def _residual_ln(h, y, g, b, tm=512):
    t, d = h.shape
    row = lambda i: (i, 0)
    fixed = lambda i: (0, 0)
    return pl.pallas_call(
        _residual_ln_kernel,
        grid=(t // tm,),
        in_specs=[pl.BlockSpec((tm, d), row), pl.BlockSpec((tm, d), row), pl.BlockSpec((1, d), fixed),
                  pl.BlockSpec((1, d), fixed)],
        out_specs=[pl.BlockSpec((tm, d), row), pl.BlockSpec((tm, d), row)],
        out_shape=[jax.ShapeDtypeStruct((t, d), F32), jax.ShapeDtypeStruct((t, d), BF16)],
        compiler_params=_params(("parallel",)),
        name="residual_ln",
    )(h, y, g.reshape(1, d), b.reshape(1, d))


def _log_sigmoid(x):
    return jnp.minimum(x, 0.0) - jnp.log(1.0 + jnp.exp(-jnp.abs(x)))


def _dot_nt(a, b):
    return lax.dot_general(a, b, (((1,), (1,)), ((), ())), preferred_element_type=F32)


def _dot_tn(a, b, precision=None):
    return lax.dot_general(a, b, (((0,), (0,)), ((), ())), precision=precision, preferred_element_type=F32)


def _head_rms(x, gain):
    return x * lax.rsqrt(jnp.mean(x * x, axis=1, keepdims=True) + HEAD_NORM_EPS) * gain


def _mixer_kernel(big_ref, small_ref, convw_ref, gbias_ref, w2_ref, glab_ref, mng_ref, gng_ref, o_ref,
                  ext_s, qk_s, c_s, n_s, m_s, s_s, *, tb):
    i = pl.program_id(0)
    nqk = 2 * MLSTM_HEADS * MLSTM_DK
    tail = CONV_WIDTH - 1

    @pl.when(i == 0)
    def _():
        ext_s[0:8, :] = jnp.zeros((8, nqk), F32)
        c_s[...] = jnp.zeros_like(c_s)
        n_s[...] = jnp.zeros_like(n_s)
        m_s[...] = jnp.zeros_like(m_s)
        s_s[...] = jnp.zeros_like(s_s)

    ext_s[8:8 + tb, :] = big_ref[:, E_MQ:E_MQ + nqk].astype(F32)
    conv = ext_s[8:8 + tb, :] * convw_ref[tail:tail + 1, :]
    for j in range(tail):
        conv = conv + ext_s[8 - tail + j:8 - tail + j + tb, :] * convw_ref[j:j + 1, :]
    qk_s[...] = conv * jax.nn.sigmoid(conv)
    ext_s[0:8, :] = ext_s[tb:tb + 8, :]

    rows = lax.broadcasted_iota(jnp.int32, (CHUNK, CHUNK), 0)
    cols = lax.broadcasted_iota(jnp.int32, (CHUNK, CHUNK), 1)
    causal = rows >= cols
    ltri = causal.astype(F32)
    eye = (rows == cols).astype(F32)
    lane = lax.broadcasted_iota(jnp.int32, (CHUNK, LANES), 1)
    is_f = (lane >= S_MF) & (lane < S_MF + MLSTM_HEADS)

    def chunk(c, carry):
        r0 = pl.multiple_of(c * CHUNK, CHUNK)
        rs = pl.ds(r0, CHUNK)
        gates = small_ref[rs, :]
        pre = gates + gbias_ref[...]
        xg = jnp.where(is_f, _log_sigmoid(pre), pre)
        cum = jnp.dot(ltri, xg, precision=HIGHEST, preferred_element_type=F32)
        y = jnp.where(is_f, cum, xg)
        yt = _dot_tn(y, eye, precision=HIGHEST)

        for h in range(MLSTM_HEADS):
            q = qk_s[rs, h * MLSTM_DK:(h + 1) * MLSTM_DK]
            k = qk_s[rs, (MLSTM_HEADS + h) * MLSTM_DK:(MLSTM_HEADS + h + 1) * MLSTM_DK] * MLSTM_DK ** -0.5
            v = big_ref[rs, E_MV + h * MLSTM_DV:E_MV + (h + 1) * MLSTM_DV]
            ig_c = y[:, S_MI + h:S_MI + h + 1]
            b_c = y[:, S_MF + h:S_MF + h + 1]
            ig_r = yt[S_MI + h:S_MI + h + 1, :]
            b_r = yt[S_MF + h:S_MF + h + 1, :]
            b_last = y[CHUNK - 1:CHUNK, S_MF + h:S_MF + h + 1]
            c_prev = c_s[h]
            n_prev = n_s[h:h + 1, :]
            m_prev = m_s[h:h + 1, 0:1]
            d_log = jnp.where(causal, b_c - b_r + ig_r, -jnp.inf)
            inter = b_c + m_prev
            m_q = jnp.maximum(inter, jnp.max(d_log, axis=1, keepdims=True))
            qb = q.astype(BF16)
            w = _dot_nt(qb, k.astype(BF16)) * jnp.exp(d_log - m_q)
            s_int = jnp.exp(inter - m_q)
            num = jnp.dot(w.astype(BF16), v, preferred_element_type=F32) + s_int * jnp.dot(
                qb, c_prev.astype(BF16), preferred_element_type=F32)
            den = jnp.sum(w, axis=1, keepdims=True) + s_int * jnp.sum(q * n_prev, axis=1, keepdims=True)
            out = num / jnp.maximum(jnp.abs(den), jnp.exp(-m_q))
            a_c = b_last - b_c + ig_c
            m_loc = jnp.max(a_c, axis=0, keepdims=True)
            kw = k * jnp.exp(a_c - m_loc)
            c_loc = _dot_tn(kw.astype(BF16), v)
            n_loc = jnp.sum(kw, axis=0, keepdims=True)
            m_new = jnp.maximum(b_last + m_prev, m_loc)
            s_old = jnp.exp(b_last + m_prev - m_new)
            s_loc = jnp.exp(m_loc - m_new)
            c_s[h] = s_old * c_prev + s_loc * c_loc
            n_s[h:h + 1, :] = s_old * n_prev + s_loc * n_loc
            m_s[h:h + 1, :] = jnp.broadcast_to(m_new, (1, LANES))
            hn = _head_rms(out, mng_ref[:, h * MLSTM_DV:(h + 1) * MLSTM_DV])
            og = big_ref[rs, E_MO + h * MLSTM_DV:E_MO + (h + 1) * MLSTM_DV].astype(F32)
            o_ref[rs, h * MLSTM_DV:(h + 1) * MLSTM_DV] = (hn * jax.nn.sigmoid(og)).astype(BF16)

        z = jnp.dot(gates, w2_ref[...], precision=HIGHEST, preferred_element_type=F32) + glab_ref[...]
        log_a = _log_sigmoid(z) / GLA_GATE_TAU
        la_all = jnp.dot(ltri, log_a, precision=HIGHEST, preferred_element_type=F32)
        for h in range(GLA_HEADS):
            la = la_all[:, h * GLA_DK:(h + 1) * GLA_DK]
            la_last = la[CHUNK - 1:CHUNK, :]
            q = big_ref[rs, E_GQ + h * GLA_DK:E_GQ + (h + 1) * GLA_DK].astype(F32) * GLA_DK ** -0.5
            k = big_ref[rs, E_GK + h * GLA_DK:E_GK + (h + 1) * GLA_DK].astype(F32)
            v = big_ref[rs, E_GV + h * GLA_DV:E_GV + (h + 1) * GLA_DV]
            q_dec = (q * jnp.exp(la)).astype(BF16)
            k_dec = (k * jnp.exp(-la)).astype(BF16)
            k_end = (k * jnp.exp(la_last - la)).astype(BF16)
            attn = jnp.where(causal, _dot_nt(q_dec, k_dec), 0.0)
            st_prev = s_s[h]
            out = jnp.dot(attn.astype(BF16), v, preferred_element_type=F32) + _dot_nt(q_dec, st_prev.astype(BF16))
            s_s[h] = jnp.exp(la_last) * st_prev + _dot_tn(v, k_end)
            hn = _head_rms(out, gng_ref[:, h * GLA_DV:(h + 1) * GLA_DV])
            gr = big_ref[rs, E_GR + h * GLA_DV:E_GR + (h + 1) * GLA_DV].astype(F32)
            col = MLSTM_HEADS * MLSTM_DV + h * GLA_DV
            o_ref[rs, col:col + GLA_DV] = (hn * (gr * jax.nn.sigmoid(gr))).astype(BF16)
        return carry

    lax.fori_loop(0, tb // CHUNK, chunk, 0)


def _mixer(big, small, convw, gbias, w2pad, glab, mng, gng, tb=512):
    t = big.shape[0]
    nqk = 2 * MLSTM_HEADS * MLSTM_DK
    fixed = lambda i: (0, 0)
    return pl.pallas_call(
        functools.partial(_mixer_kernel, tb=tb),
        grid=(t // tb,),
        in_specs=[pl.BlockSpec((tb, E_BIG), lambda i: (i, 0)), pl.BlockSpec((tb, LANES), lambda i: (i, 0)),
                  pl.BlockSpec((CONV_WIDTH, nqk), fixed), pl.BlockSpec((1, LANES), fixed),
                  pl.BlockSpec((LANES, GLA_HEADS * GLA_DK), fixed), pl.BlockSpec((1, GLA_HEADS * GLA_DK), fixed),
                  pl.BlockSpec((1, MLSTM_HEADS * MLSTM_DV), fixed), pl.BlockSpec((1, GLA_HEADS * GLA_DV), fixed)],
        out_specs=pl.BlockSpec((tb, MIX_WIDTH), lambda i: (i, 0)),
        out_shape=jax.ShapeDtypeStruct((t, MIX_WIDTH), BF16),
        scratch_shapes=[pltpu.VMEM((tb + 8, nqk), F32), pltpu.VMEM((tb, nqk), F32),
                        pltpu.VMEM((MLSTM_HEADS, MLSTM_DK, MLSTM_DV), F32), pltpu.VMEM((8, MLSTM_DK), F32),
                        pltpu.VMEM((8, LANES), F32), pltpu.VMEM((GLA_HEADS, GLA_DV, GLA_DK), F32)],
        compiler_params=_params(("arbitrary",)),
    )(big, small, convw, gbias, w2pad, glab, mng, gng)


def _rope_kernel(big_ref, small_ref, pos_ref, freq_ref, sign_ref, q_ref, k_ref, qi_ref, kit_ref, wi_ref):
    pos = pos_ref[...].astype(F32)
    lane = lax.broadcasted_iota(jnp.int32, (pos.shape[0], LANES), 1)

    def tables(row):
        ang = pos * freq_ref[row:row + 1, :]
        return jnp.cos(ang), jnp.sin(ang) * sign_ref[row:row + 1, :]

    def rotate(x, cos, sin, half, period):
        up = pltpu.roll(x, LANES - half, axis=1)
        dn = pltpu.roll(x, half, axis=1)
        return x * cos + jnp.where((lane % period) < half, up, dn) * sin

    cos_a, sin_a = tables(0)
    for h in range(ATTN_HEADS):
        x = big_ref[:, O_Q + h * ATTN_DH:O_Q + (h + 1) * ATTN_DH].astype(F32)
        r = rotate(x, cos_a, sin_a, ROPE_DIM // 2, ATTN_DH) * QK_SCALE_LOG2
        q_ref[:, h * ATTN_DH:(h + 1) * ATTN_DH] = r.astype(BF16)
    for h in range(ATTN_KV_HEADS):
        x = big_ref[:, O_K + h * ATTN_DH:O_K + (h + 1) * ATTN_DH].astype(F32)
        k_ref[:, h * ATTN_DH:(h + 1) * ATTN_DH] = rotate(x, cos_a, sin_a, ROPE_DIM // 2, ATTN_DH).astype(BF16)
    cos_i, sin_i = tables(1)
    for h in range(IDX_WIDTH // LANES):
        x = big_ref[:, O_QI + h * LANES:O_QI + (h + 1) * LANES].astype(F32)
        qi_ref[:, h * LANES:(h + 1) * LANES] = rotate(x, cos_i, sin_i, IDX_ROPE_DIM // 2, IDX_DH).astype(BF16)
    cos_k, sin_k = tables(2)
    sm = small_ref[...]
    ki = rotate(sm, cos_k, sin_k, IDX_ROPE_DIM // 2, IDX_DH)[:, :IDX_DH].astype(BF16)
    eye = (lax.broadcasted_iota(jnp.int32, (IDX_DH, IDX_DH), 0)
           == lax.broadcasted_iota(jnp.int32, (IDX_DH, IDX_DH), 1)).astype(BF16)
    kit_ref[...] = _dot_nt(eye, ki).astype(BF16)
    wi_ref[...] = pltpu.roll(sm, LANES - IDX_DH, axis=1) * (IDX_HEADS ** -0.5 * IDX_DH ** -0.5)


def _rope_tables():
    def pattern(rot_dim, period, width):
        half = rot_dim // 2
        inv = 1.0 / (ROPE_THETA ** (jnp.arange(half, dtype=F32) / half))
        l = np.arange(LANES)
        inside = ((l % period) < rot_dim) & (l < width)
        freq = jnp.where(inside, inv[(l % period) % half], 0.0)
        sign = np.where(inside, np.where((l % period) < half, -1.0, 1.0), 0.0)
        return freq, jnp.asarray(sign, F32)

    fa, sa = pattern(ROPE_DIM, ATTN_DH, LANES)
    fi, si = pattern(IDX_ROPE_DIM, IDX_DH, LANES)
    fk, sk = pattern(IDX_ROPE_DIM, IDX_DH, IDX_DH)
    z = jnp.zeros((5, LANES), F32)
    return jnp.concatenate([jnp.stack([fa, fi, fk]), z]), jnp.concatenate([jnp.stack([sa, si, sk]), z])


def _rope(big, small, pos, tm=512):
    t = big.shape[0]
    freq, sign = _rope_tables()
    row = lambda i: (i, 0)
    fixed = lambda i: (0, 0)
    return pl.pallas_call(
        _rope_kernel,
        grid=(t // tm,),
        in_specs=[pl.BlockSpec((tm, O_BIG), row), pl.BlockSpec((tm, LANES), row), pl.BlockSpec((tm, 1), row),
                  pl.BlockSpec((8, LANES), fixed), pl.BlockSpec((8, LANES), fixed)],
        out_specs=[pl.BlockSpec((tm, ATTN_WIDTH), row), pl.BlockSpec((tm, KV_WIDTH), row),
                   pl.BlockSpec((tm, IDX_WIDTH), row), pl.BlockSpec((IDX_DH, tm), lambda i: (0, i)),
                   pl.BlockSpec((tm, LANES), row)],
        out_shape=[jax.ShapeDtypeStruct((t, ATTN_WIDTH), BF16), jax.ShapeDtypeStruct((t, KV_WIDTH), BF16),
                   jax.ShapeDtypeStruct((t, IDX_WIDTH), BF16), jax.ShapeDtypeStruct((IDX_DH, t), BF16),
                   jax.ShapeDtypeStruct((t, LANES), F32)],
        compiler_params=_params(("parallel",)),
        name="rope",
    )(big, small, pos, freq, sign)


def _dsa_kernel(q_ref, qi_ref, wi_ref, kit_ref, k_ref, v_ref, o_ref,
                keys_s, hi_s, cut_s, bias_s, p_s, alpha_s, m_s, acc_s, *, tq, tk, top_k, n_key_slots):
    qb = pl.program_id(0)
    n_tiles = (qb * tq + tq - 1) // tk + 1

    w = wi_ref[...]
    qpos = qb * tq + lax.broadcasted_iota(jnp.int32, (tq, tk), 0)
    kiota = lax.broadcasted_iota(jnp.int32, (tq, tk), 1)

    def score_tile(c, carry):
        c0 = pl.multiple_of(c * tk, tk)
        kt = kit_ref[:, pl.ds(c0, tk)]
        sc = jnp.zeros((tq, tk), F32)
        for h in range(IDX_HEADS):
            lg = jnp.dot(qi_ref[:, h * IDX_DH:(h + 1) * IDX_DH], kt, preferred_element_type=F32)
            sc = sc + jnp.maximum(lg, 0.0) * w[:, h:h + 1]
        bits = pltpu.bitcast(sc, jnp.int32)
        key = bits ^ ((bits >> 31) & 0x7FFFFFFF)
        key = jnp.where(kiota + c0 <= qpos, key, INT_MIN)
        keys_s[:, pl.ds(c0, tk)] = key
        hi_s[:, pl.ds(c0, tk)] = (key >> 16).astype(jnp.int16)
        return carry

    lax.fori_loop(0, n_tiles, score_tile, 0)

    n_sweeps = (n_tiles + SWEEP - 1) // SWEEP

    def pad_tile(c, carry):
        c0 = pl.multiple_of(c * tk, tk)
        keys_s[:, pl.ds(c0, tk)] = jnp.full((tq, tk), INT_MIN, jnp.int32)
        hi_s[:, pl.ds(c0, tk)] = jnp.full((tq, tk), -(2 ** 15), jnp.int16)
        return carry

    lax.fori_loop(n_tiles, n_sweeps * SWEEP, pad_tile, 0)

    def lane_total(acc):
        ones_sq = jnp.ones((LANES, LANES), BF16)
        return jnp.dot(acc, ones_sq, preferred_element_type=F32).astype(jnp.int32)

    def count_ge(cand):
        def body(c, acc):
            c0 = pl.multiple_of(c * (SWEEP * tk), SWEEP * tk)
            for u in range(SWEEP * tk // LANES):
                ge = keys_s[:, pl.ds(c0 + u * LANES, LANES)] >= cand
                acc = acc + ge.astype(jnp.int32)
            return acc

        acc = lax.fori_loop(0, n_sweeps, body, jnp.zeros((tq, LANES), jnp.int32))
        return lane_total(acc.astype(F32).astype(BF16))

    def count_hi_ge(cand):
        c16 = cand.astype(jnp.int16)
        one = jnp.ones((tq, LANES), BF16)
        nil = jnp.zeros((tq, LANES), BF16)

        def body(c, acc):
            c0 = pl.multiple_of(c * (SWEEP * tk), SWEEP * tk)
            for u in range(SWEEP * tk // LANES):
                acc = acc + jnp.where(hi_s[:, pl.ds(c0 + u * LANES, LANES)] >= c16, one, nil)
            return acc

        acc = lax.fori_loop(0, n_sweeps, body, nil)
        return lane_total(acc)

    def bit_step(count_fn, shift, b, ubase, kept):
        ucand = ubase | jnp.left_shift(jnp.int32(1), 31 - b)
        cnt = count_fn((ucand ^ INT_MIN) >> shift)
        ok = cnt >= top_k
        return jnp.where(ok, ucand, ubase), jnp.where(ok, cnt, kept)

    ubase = jnp.zeros((tq, LANES), jnp.int32)
    kept = jnp.full((tq, LANES), n_sweeps * SWEEP * tk, jnp.int32)
    ubase, kept = lax.fori_loop(0, 16, lambda b, st: bit_step(count_hi_ge, 16, b, *st), (ubase, kept))

    def unsettled(state):
        b, _, kept = state
        return jnp.logical_and(b < 32, jnp.max(jnp.abs(kept - top_k)) > 0)

    def four_bits(state):
        b, ubase, kept = state
        for j in range(4):
            ubase, kept = bit_step(count_ge, 0, b + j, ubase, kept)
        return b + 4, ubase, kept

    _, ubase, kept = lax.while_loop(unsettled, four_bits, (jnp.int32(16), ubase, kept))
    thr = ubase ^ INT_MIN
    thr = jnp.maximum(thr, INT_MIN + 1)
    thr_t = jnp.tile(thr, (1, tk // LANES))

    cut_s[...] = jnp.full(cut_s.shape, 2 ** 31 - 1, jnp.int32)
    lane_pos = lax.broadcasted_iota(jnp.int32, (tq, LANES), 1)

    @pl.when(jnp.max(jnp.abs(kept - top_k)) > 0)
    def _():
        need = top_k - count_ge(thr + 1)

        def count_tied_before(pos):
            def body(c, acc):
                c0 = pl.multiple_of(c * (SWEEP * tk), SWEEP * tk)
                for u in range(SWEEP * tk // LANES):
                    tied = keys_s[:, pl.ds(c0 + u * LANES, LANES)] == thr
                    acc = acc + jnp.where(tied, (lane_pos + (c0 + u * LANES) < pos).astype(jnp.int32), 0)
                return acc

            acc = lax.fori_loop(0, n_sweeps, body, jnp.zeros((tq, LANES), jnp.int32))
            return lane_total(acc.astype(F32).astype(BF16))

        pos_bits = max(1, (n_key_slots - 1).bit_length())

        def pos_step(b, cut):
            cand = cut | jnp.left_shift(jnp.int32(1), pos_bits - 1 - b)
            return jnp.where(count_tied_before(cand) < need, cand, cut)

        cut_s[...] = lax.fori_loop(0, pos_bits, pos_step, jnp.zeros((tq, LANES), jnp.int32))

    cut_t = jnp.tile(cut_s[...], (1, tk // LANES))

    m_s[...] = jnp.full_like(m_s, NEG_BIG)
    acc_s[...] = jnp.zeros_like(acc_s)
    ones = jnp.ones((tk, ATTN_DH), BF16)

    def flash_tile(c, carry):
        c0 = pl.multiple_of(c * tk, tk)
        key = keys_s[:, pl.ds(c0, tk)]
        keep = (key > thr_t) | ((key == thr_t) & (kiota + c0 <= cut_t))
        bias_s[...] = jnp.where(keep, 0.0, NEG_BIG)
        for g in range(ATTN_KV_HEADS):
            qg = jnp.concatenate(
                [q_ref[:, (g * Q_PER_KV + r) * ATTN_DH:(g * Q_PER_KV + r + 1) * ATTN_DH] for r in range(Q_PER_KV)],
                axis=0)
            s = _dot_nt(qg, k_ref[pl.ds(c0, tk), g * ATTN_DH:(g + 1) * ATTN_DH])
            for r in range(Q_PER_KV):
                rows = slice(r * tq, (r + 1) * tq)
                sr = s[rows, :] + bias_s[...]
                m_prev = m_s[g, rows, :]
                m_new = jnp.maximum(m_prev, jnp.max(sr, axis=1, keepdims=True))
                p_s[g, rows, :] = jnp.exp2(sr - jnp.tile(m_new, (1, tk // LANES))).astype(BF16)
                alpha_s[g, rows, :] = jnp.exp2(m_prev - m_new)
                m_s[g, rows, :] = m_new
            v_ext = jnp.concatenate([v_ref[pl.ds(c0, tk), g * ATTN_DH:(g + 1) * ATTN_DH], ones], axis=1)
            pv = jnp.dot(p_s[g], v_ext, preferred_element_type=F32)
            acc_s[g] = jnp.tile(alpha_s[g], (1, 2)) * acc_s[g] + pv
        return carry

    lax.fori_loop(0, n_tiles, flash_tile, 0)

    for g in range(ATTN_KV_HEADS):
        a = acc_s[g]
        res = a[:, :ATTN_DH] / a[:, ATTN_DH:]
        for r in range(Q_PER_KV):
            hcol = (g * Q_PER_KV + r) * ATTN_DH
            o_ref[:, hcol:hcol + ATTN_DH] = res[r * tq:(r + 1) * tq, :].astype(BF16)


def _dsa(q, qi, wi, kit, k, big, tq=128, tk=512):
    t = q.shape[0]
    tcap = -(-t // (SWEEP * tk)) * (SWEEP * tk)
    assert tcap // LANES <= 256, "per-lane key counts are accumulated in bf16"
    top_k = min(INDEX_TOPK, t // 4)
    v_blk = O_V // KV_WIDTH
    qrow = lambda i: (i, 0)
    once = pl.Buffered(1)
    rq = Q_PER_KV * tq
    return pl.pallas_call(
        functools.partial(_dsa_kernel, tq=tq, tk=tk, top_k=top_k, n_key_slots=tcap),
        grid=(t // tq,),
        in_specs=[pl.BlockSpec((tq, ATTN_WIDTH), qrow), pl.BlockSpec((tq, IDX_WIDTH), qrow),
                  pl.BlockSpec((tq, LANES), qrow),
                  pl.BlockSpec((IDX_DH, t), lambda i: (0, 0), pipeline_mode=once),
                  pl.BlockSpec((t, KV_WIDTH), lambda i: (0, 0), pipeline_mode=once),
                  pl.BlockSpec((t, KV_WIDTH), lambda i: (0, v_blk), pipeline_mode=once)],
        out_specs=pl.BlockSpec((tq, ATTN_WIDTH), qrow),
        out_shape=jax.ShapeDtypeStruct((t, ATTN_WIDTH), BF16),
        scratch_shapes=[pltpu.VMEM((tq, tcap + LANES), jnp.int32), pltpu.VMEM((tq, tcap + LANES), jnp.int16),
                        pltpu.VMEM((tq, LANES), jnp.int32),
                        pltpu.VMEM((tq, tk), F32), pltpu.VMEM((ATTN_KV_HEADS, rq, tk), BF16),
                        pltpu.VMEM((ATTN_KV_HEADS, rq, LANES), F32), pltpu.VMEM((ATTN_KV_HEADS, rq, LANES), F32),
                        pltpu.VMEM((ATTN_KV_HEADS, rq, 2 * ATTN_DH), F32)],
        compiler_params=pltpu.CompilerParams(dimension_semantics=("parallel",), vmem_limit_bytes=DSA_VMEM_LIMIT),
        name="dsa_attention",
    )(q, qi, wi, kit, k, big)


def _pad_cols(w, width):
    return jnp.pad(w, ((0, 0), (0, width - w.shape[1])))


def _even_weights(w_in, igate_b, fgate_b, gla_w2, gla_b):
    cuts = np.cumsum([512, 512, 1024, 4, 4, 1024, 512, 512, 1024, 16, 1024])[:-1]
    mq, mk, mv, mi, mf, mo, gq, gk, gv, glr, gr = jnp.split(w_in, [int(c) for c in cuts], axis=1)
    w_big = jnp.concatenate([mq, mk, mv, mo, gq, gk, gv, gr], axis=1).astype(BF16)
    w_small = _pad_cols(jnp.concatenate([mi, mf, glr], axis=1), LANES).astype(BF16)
    gbias = _pad_cols(jnp.concatenate([igate_b, fgate_b])[None, :], LANES)
    w2pad = jnp.zeros((LANES, GLA_HEADS * GLA_DK), F32).at[S_GLR:S_GLR + GLA_GATE_RANK].set(gla_w2)
    return w_big, w_small, gbias, w2pad, gla_b[None, :]


def _odd_weights(w_in):
    cuts = np.cumsum([ATTN_WIDTH, KV_WIDTH, KV_WIDTH, IDX_WIDTH, IDX_DH, IDX_HEADS])[:-1]
    q, k, v, qi, ki, wi = jnp.split(w_in, [int(c) for c in cuts], axis=1)
    w_big = jnp.concatenate([q, k, v, qi], axis=1).astype(BF16)
    w_small = _pad_cols(jnp.concatenate([ki, wi], axis=1), LANES).astype(BF16)
    return w_big, w_small


def _even_mixer(hb, w_in, conv_w, igate_b, fgate_b, gla_w2, gla_b, mng, gng):
    w_big, w_small, gbias, w2pad, glab = _even_weights(w_in, igate_b, fgate_b, gla_w2, gla_b)
    big = _matmul(hb, w_big, BF16, 1024, 1024)
    small = _matmul(hb, w_small, F32, 1024, LANES)
    return _mixer(big, small, conv_w, gbias, w2pad, glab, mng[None, :], gng[None, :])


def _odd_mixer(hb, pos, w_in):
    w_big, w_small = _odd_weights(w_in)
    big = _matmul(hb, w_big, BF16, 1024, 1024)
    small = _matmul(hb, w_small, F32, 1024, LANES)
    q, k, qi, kit, wi = _rope(big, small, pos)
    return _dsa(q, qi, wi, kit, k, big)


def kernel(x, positions, ab_w_in, ab_conv_w, ab_igate_b, ab_fgate_b, ab_gla_gate_w2, ab_gla_gate_b, ab_mlstm_norm_g, ab_gla_norm_g, ab_w_out, c_w_in, c_w_out, router_w, router_bias, moe_w_gate, moe_w_up, moe_w_down, ln_mix_g, ln_mix_b, ln_ffn_g, ln_ffn_b):
    bsz, t, d = x.shape
    outs = []
    rwt = router_w.T
    rbias = router_bias[:, None]
    flat = lambda w: w.reshape((DEPTH * N_EXPERTS,) + w.shape[2:])
    wg, wu, wd = _cast_expert_weights(flat(moe_w_gate), flat(moe_w_up), flat(moe_w_down))
    for bi in range(bsz):
        h = x[bi]
        hb = h.astype(BF16)
        pos = positions[bi][:, None]
        for layer in range(DEPTH):
            j = layer // 2
            if layer % 2 == 0:
                mix = _even_mixer(hb, ab_w_in[j], ab_conv_w[j], ab_igate_b[j], ab_fgate_b[j], ab_gla_gate_w2[j],
                                  ab_gla_gate_b[j], ab_mlstm_norm_g[j], ab_gla_norm_g[j])
                w_out = ab_w_out[j]
            else:
                mix = _odd_mixer(hb, pos, c_w_in[j])
                w_out = c_w_out[j]
            h = _outproj_ln(mix, w_out.astype(BF16), h, ln_mix_g[layer], ln_mix_b[layer])
            grp, comb_t = _router(h, rwt, rbias)
            y = _moe(h, grp[0], comb_t.T, wg, wu, wd, layer * N_EXPERTS)
            h, hb = _residual_ln(h, y, ln_ffn_g[layer], ln_ffn_b[layer])
        outs.append(h)
    return jnp.stack(outs)
```
